```python
import math
import jax, jax.numpy as jnp
from jax import lax
import numpy as np

D_MODEL = 2048
BATCH = 2
SEQ = 16384
DEPTH = 1

CHUNK = 64
N_MEM = 256
QBLK = 128
ROPE_THETA = 10000.0
LN_EPS = 1e-5
DA_HEADS = 4
DA_HEAD_DIM = 128
DA_V_DIM = 2 * DA_HEAD_DIM
DA_WIDTH = DA_HEADS * 2 * DA_HEAD_DIM
SB_HEADS = 4
SB_HEAD_DIM = 256
SB_WIDTH = SB_HEADS * SB_HEAD_DIM
N_BRANCH = 2
MEM_HEADS = 4
MEM_HEAD_DIM = D_MODEL // MEM_HEADS
D_FF = ((8 * D_MODEL // 3 + 255) // 256) * 256
IN_WIDTH = 3 * DA_WIDTH + 3 * SB_WIDTH + N_BRANCH * D_MODEL
SPLITS = [int(v) for v in np.cumsum([DA_WIDTH] * 3 + [SB_WIDTH] * 3)]
DEEPNORM_ALPHA = (2.0 * DEPTH) ** 0.25
DEEPNORM_BETA = (8.0 * DEPTH) ** -0.25

kernel_name = "hybrid_diff_stickbreak_gated_deepnorm"


def layer_norm(x, g, b):
    x32 = x.astype(jnp.float32)
    mu = jnp.mean(x32, axis=-1, keepdims=True)
    var = jnp.mean(jnp.square(x32 - mu), axis=-1, keepdims=True)
    y = (x32 - mu) * lax.rsqrt(var + LN_EPS) * g.astype(jnp.float32) + b.astype(jnp.float32)
    return y.astype(x.dtype)


def rope_tables(seq, dim):
    inv = ROPE_THETA ** (-jnp.arange(0, dim, 2, dtype=jnp.float32) / dim)
    ang = jnp.arange(seq, dtype=jnp.float32)[:, None] * inv[None, :]
    return jnp.cos(ang), jnp.sin(ang)


def apply_rope(t, cos, sin):
    t32 = t.astype(jnp.float32)
    h = t.shape[-1] // 2
    t1, t2 = t32[..., :h], t32[..., h:]
    c = cos[None, :, None, :]
    s = sin[None, :, None, :]
    return jnp.concatenate([t1 * c - t2 * s, t2 * c + t1 * s], axis=-1).astype(t.dtype)


def reverse_exclusive_cumsum(x):
    L = x.shape[-1]
    nk = L // QBLK
    xb = x.reshape(x.shape[:-1] + (nk, QBLK))
    tri = (jnp.arange(QBLK)[:, None] > jnp.arange(QBLK)[None, :]).astype(x.dtype)
    within = jnp.einsum('...cj,js->...cs', xb, tri, precision=lax.Precision.HIGHEST)
    tot = jnp.sum(xb, axis=-1)
    later = (jnp.arange(nk)[:, None] > jnp.arange(nk)[None, :]).astype(x.dtype)
    carry = jnp.einsum('...d,dc->...c', tot, later, precision=lax.Precision.HIGHEST)
    return (within + carry[..., None]).reshape(x.shape)


def diff_attention(q, k, v, lam, norm_g, lambda_init):
    B, S = q.shape[0], q.shape[1]
    nb = S // QBLK
    scale = DA_HEAD_DIM ** -0.5
    outs = []
    for i in range(nb):
        L = (i + 1) * QBLK
        qblk = q[:, i * QBLK:L]
        kk, vv = k[:, :L], v[:, :L]
        qpos = i * QBLK + jnp.arange(QBLK)
        allowed = (jnp.arange(L) // CHUNK)[None, :] <= (qpos // CHUNK)[:, None]
        s = jnp.einsum('bqhd,bkhd->bhqk', qblk, kk).astype(jnp.float32) * scale
        s = jnp.where(allowed, s, -jnp.inf)
        p = jax.nn.softmax(s, axis=-1).reshape(B, DA_HEADS, 2, QBLK, L)
        w = p[:, :, 0] - lam * p[:, :, 1]
        outs.append(jnp.einsum('bhqk,bkhd->bqhd', w.astype(vv.dtype), vv))
    o = jnp.concatenate(outs, axis=1).astype(jnp.float32)
    o = o * lax.rsqrt(jnp.mean(jnp.square(o), axis=-1, keepdims=True) + LN_EPS)
    o = o * norm_g.astype(jnp.float32) * (1.0 - lambda_init)
    return o.reshape(B, S, DA_WIDTH).astype(q.dtype)


def stick_breaking_attention(q, k, v):
    B, S = q.shape[0], q.shape[1]
    nb = S // QBLK
    scale = SB_HEAD_DIM ** -0.5
    outs = []
    for i in range(nb):
        L = (i + 1) * QBLK
        qblk = q[:, i * QBLK:L]
        kk, vv = k[:, :L], v[:, :L]
        qpos = i * QBLK + jnp.arange(QBLK)
        strict = jnp.arange(L)[None, :] < qpos[:, None]
        z = jnp.einsum('bqhd,bkhd->bhqk', qblk, kk).astype(jnp.float32) * scale
        log_1mb = jnp.where(strict, jax.nn.log_sigmoid(-z), 0.0)
        suffix = reverse_exclusive_cumsum(log_1mb)
        w = jnp.where(strict, jnp.exp(jax.nn.log_sigmoid(z) + suffix), 0.0)
        outs.append(jnp.einsum('bhqk,bkhd->bqhd', w.astype(vv.dtype), vv))
    return jnp.concatenate(outs, axis=1).reshape(B, S, SB_WIDTH)


def memory_cross_attention(x, mem, w_q, w_k, w_v, w_o):
    B, S = x.shape[0], x.shape[1]
    M = mem.shape[1]
    q = (x @ w_q).reshape(B, S, MEM_HEADS, MEM_HEAD_DIM)
    k = (mem @ w_k).reshape(B, M, MEM_HEADS, MEM_HEAD_DIM)
    v = (mem @ w_v).reshape(B, M, MEM_HEADS, MEM_HEAD_DIM)
    s = jnp.einsum('bqhd,bmhd->bhqm', q, k).astype(jnp.float32) * (MEM_HEAD_DIM ** -0.5)
    p = jax.nn.softmax(s, axis=-1)
    o = jnp.einsum('bhqm,bmhd->bqhd', p.astype(v.dtype), v).reshape(B, S, D_MODEL)
    return o @ w_o


def setup_inputs(seed: int = 0) -> dict:
    key = jax.random.key(seed)
    ks = jax.random.split(key, 32)
    f = jnp.float32
    n = lambda k, shape, s: jax.random.normal(k, shape, f) * s
    L, D = DEPTH, D_MODEL
    return {
        "x": n(ks[0], (BATCH, SEQ, D), 1.0),
        "mem": n(ks[1], (BATCH, N_MEM, D), 1.0),
        "w_in": n(ks[2], (L, D, IN_WIDTH), D ** -0.5),
        "b_gate": n(ks[3], (L, N_BRANCH * D), 0.02),
        "lam_q1": n(ks[4], (L, DA_HEAD_DIM), 0.1),
        "lam_k1": n(ks[5], (L, DA_HEAD_DIM), 0.1),
        "lam_q2": n(ks[6], (L, DA_HEAD_DIM), 0.1),
        "lam_k2": n(ks[7], (L, DA_HEAD_DIM), 0.1),
        "da_norm_g": 1.0 + n(ks[8], (L, DA_V_DIM), 0.02),
        "w_proj_a": n(ks[9], (L, DA_WIDTH, D), DA_WIDTH ** -0.5),
        "w_proj_b": n(ks[10], (L, SB_WIDTH, D), SB_WIDTH ** -0.5),
        "w_out": n(ks[11], (L, D, D), D ** -0.5 * DEEPNORM_BETA),
        "ln1_g": 1.0 + n(ks[12], (L, D), 0.02),
        "ln1_b": n(ks[13], (L, D), 0.02),
        "w_mq": n(ks[14], (L, D, D), D ** -0.5),
        "w_mk": n(ks[15], (L, D, D), D ** -0.5),
        "w_mv": n(ks[16], (L, D, D), D ** -0.5),
        "w_mo": n(ks[17], (L, D, D), D ** -0.5 * DEEPNORM_BETA),
        "ln2_g": 1.0 + n(ks[18], (L, D), 0.02),
        "ln2_b": n(ks[19], (L, D), 0.02),
        "w_gate": n(ks[20], (L, D, D_FF), D ** -0.5),
        "w_up": n(ks[21], (L, D, D_FF), D ** -0.5),
        "w_down": n(ks[22], (L, D_FF, D), D_FF ** -0.5 * DEEPNORM_BETA),
        "ln3_g": 1.0 + n(ks[23], (L, D), 0.02),
        "ln3_b": n(ks[24], (L, D), 0.02),
    }


def reference(x, mem, w_in, b_gate, lam_q1, lam_k1, lam_q2, lam_k2, da_norm_g,
              w_proj_a, w_proj_b, w_out, ln1_g, ln1_b, w_mq, w_mk, w_mv, w_mo,
              ln2_g, ln2_b, w_gate, w_up, w_down, ln3_g, ln3_b):
    B, S = x.shape[0], x.shape[1]
    cos, sin = rope_tables(S, DA_HEAD_DIM)
    for l in range(DEPTH):
        lambda_init = 0.8 - 0.6 * math.exp(-0.3 * l)
        h = x @ w_in[l]
        qa, ka, va, qb, kb, vb, g = jnp.split(h, SPLITS, axis=-1)
        qa = apply_rope(qa.reshape(B, S, 2 * DA_HEADS, DA_HEAD_DIM), cos, sin)
        ka = apply_rope(ka.reshape(B, S, 2 * DA_HEADS, DA_HEAD_DIM), cos, sin)
        va = va.reshape(B, S, DA_HEADS, DA_V_DIM)
        lam = (jnp.exp(jnp.sum(lam_q1[l].astype(jnp.float32) * lam_k1[l].astype(jnp.float32)))
               - jnp.exp(jnp.sum(lam_q2[l].astype(jnp.float32) * lam_k2[l].astype(jnp.float32)))
               + lambda_init)
        ya = diff_attention(qa, ka, va, lam, da_norm_g[l], lambda_init)
        yb = stick_breaking_attention(qb.reshape(B, S, SB_HEADS, SB_HEAD_DIM),
                                      kb.reshape(B, S, SB_HEADS, SB_HEAD_DIM),
                                      vb.reshape(B, S, SB_HEADS, SB_HEAD_DIM))
        gates = jax.nn.sigmoid(g + b_gate[l])
        g_a, g_b = gates[..., :D_MODEL], gates[..., D_MODEL:]
        mixed = g_a * (ya @ w_proj_a[l]) + g_b * (yb @ w_proj_b[l])
        x = layer_norm(DEEPNORM_ALPHA * x + mixed @ w_out[l], ln1_g[l], ln1_b[l])
        c = memory_cross_attention(x, mem, w_mq[l], w_mk[l], w_mv[l], w_mo[l])
        x = layer_norm(DEEPNORM_ALPHA * x + c, ln2_g[l], ln2_b[l])
        f = (jax.nn.silu(x @ w_gate[l]) * (x @ w_up[l])) @ w_down[l]
        x = layer_norm(DEEPNORM_ALPHA * x + f, ln3_g[l], ln3_b[l])
    return x
```

```python
import functools
import math

import jax
import jax.numpy as jnp
from jax import lax
from jax.experimental import pallas as pl
from jax.experimental.pallas import tpu as pltpu

F32 = jnp.float32
BF16 = jnp.bfloat16

CHUNK = 64
ROPE_THETA = 10000.0
LN_EPS = 1e-5
DA_HEADS = 4
DA_HEAD_DIM = 128
DA_V_DIM = 2 * DA_HEAD_DIM
SB_HEADS = 4
SB_HEAD_DIM = 256
MEM_HEADS = 4
N_BRANCH = 2

LANES = 128
SUBLANES = 8
VMEM_BYTES = 64 * 1024 * 1024

ATTN_BLK = 512


def _vmem_limit(nbytes):
    return int(min(nbytes, VMEM_BYTES - 6 * 1024 * 1024))


def _layer_norm(r, g, b):
    mu = jnp.mean(r, axis=-1, keepdims=True)
    d = r - mu
    var = jnp.mean(d * d, axis=-1, keepdims=True)
    return d * lax.rsqrt(var + LN_EPS) * g + b


def _proj_rope_kernel(x_ref, w_ref, cos_ref, sin_ref, o_ref):
    acc = jnp.dot(x_ref[...], w_ref[...], preferred_element_type=F32)
    cos = cos_ref[...]
    sin = sin_ref[...]
    n_heads = acc.shape[1] // DA_HEAD_DIM
    for h in range(n_heads):
        t = acc[:, h * DA_HEAD_DIM:(h + 1) * DA_HEAD_DIM]
        rot = pltpu.roll(t, DA_HEAD_DIM // 2, axis=1)
        o_ref[:, h * DA_HEAD_DIM:(h + 1) * DA_HEAD_DIM] = (t * cos + rot * sin).astype(o_ref.dtype)


def _proj_plain_kernel(x_ref, w_ref, o_ref):
    o_ref[...] = jnp.dot(x_ref[...], w_ref[...], preferred_element_type=F32).astype(o_ref.dtype)


def _proj_gate_kernel(x_ref, w_ref, b_ref, o_ref):
    acc = jnp.dot(x_ref[...], w_ref[...], preferred_element_type=F32)
    o_ref[...] = jax.nn.sigmoid(acc + b_ref[...]).astype(o_ref.dtype)


def _projection(x2d, w, *, out_dtype, tm, tn, name, rope=None, bias=None, seq=None):
    t, d = x2d.shape
    n = w.shape[1]
    grid = (t // tm, n // tn)
    in_specs = [pl.BlockSpec((tm, d), lambda i, j: (i, 0)),
                pl.BlockSpec((d, tn), lambda i, j: (0, j))]
    args = [x2d, w]
    if rope is not None:
        kern = _proj_rope_kernel
        per_seq = seq // tm
        tab_spec = pl.BlockSpec((tm, DA_HEAD_DIM), lambda i, j: (i % per_seq, 0))
        in_specs += [tab_spec, tab_spec]
        args += list(rope)
    elif bias is not None:
        kern = _proj_gate_kernel
        in_specs.append(pl.BlockSpec((1, tn), lambda i, j: (0, j)))
        args.append(bias)
    else:
        kern = _proj_plain_kernel
    return pl.pallas_call(
        kern,
        out_shape=jax.ShapeDtypeStruct((t, n), out_dtype),
        grid=grid,
        in_specs=in_specs,
        out_specs=pl.BlockSpec((tm, tn), lambda i, j: (i, j)),
        compiler_params=pltpu.CompilerParams(
            dimension_semantics=("parallel", "arbitrary"),
            vmem_limit_bytes=_vmem_limit(48 * 1024 * 1024)),
        name=name,
    )(*args)


def _diff_attn_kernel(lq1_ref, lk1_ref, lq2_ref, lk2_ref, g_ref, q_ref, k_ref, v_ref, o_ref,
                      m_sc, l_sc, acc_sc, *, blk, scale, lambda_init):
    i = pl.program_id(2)
    q = q_ref[0]
    dh = DA_HEAD_DIM

    m_sc[...] = jnp.full(m_sc.shape, -jnp.inf, F32)
    l_sc[...] = jnp.zeros(l_sc.shape, F32)
    acc_sc[...] = jnp.zeros(acc_sc.shape, F32)

    def step(j, masked):
        start = pl.multiple_of(j * blk, blk)
        k = k_ref[0, pl.ds(start, blk), :]
        v = v_ref[0, pl.ds(start, blk), :]
        if masked:
            row = lax.broadcasted_iota(jnp.int32, (blk, blk), 0)
            col = lax.broadcasted_iota(jnp.int32, (blk, blk), 1)
            allowed = (col // CHUNK) <= (row // CHUNK)
        for c in range(2):
            s = lax.dot_general(q[:, c * dh:(c + 1) * dh], k[:, c * dh:(c + 1) * dh],
                                (((1,), (1,)), ((), ())), preferred_element_type=F32) * scale
            if masked:
                s = jnp.where(allowed, s, -jnp.inf)
            m_prev = m_sc[c]
            m_new = jnp.maximum(m_prev, jnp.max(s, axis=-1, keepdims=True))
            alpha = jnp.exp(m_prev - m_new)
            p = jnp.exp(s - m_new)
            l_sc[c] = alpha * l_sc[c] + jnp.sum(p, axis=-1, keepdims=True)
            acc_sc[c] = alpha * acc_sc[c] + jnp.dot(p.astype(BF16), v, preferred_element_type=F32)
            m_sc[c] = m_new

    def loop_body(j, carry):
        step(j, False)
        return carry

    lax.fori_loop(0, i, loop_body, 0)
    step(i, True)

    lam = (jnp.exp(jnp.sum(lq1_ref[...] * lk1_ref[...], axis=-1, keepdims=True))
           - jnp.exp(jnp.sum(lq2_ref[...] * lk2_ref[...], axis=-1, keepdims=True))
           + lambda_init)
    o = acc_sc[0] / l_sc[0] - lam * (acc_sc[1] / l_sc[1])
    o = o * lax.rsqrt(jnp.mean(o * o, axis=-1, keepdims=True) + LN_EPS)
    o = o * g_ref[...] * (1.0 - lambda_init)
    o_ref[0] = o.astype(o_ref.dtype)


def _diff_attention(qa, ka, va, lam_params, norm_g, *, blk, lambda_init):
    b, s, width = qa.shape
    hw = 2 * DA_HEAD_DIM
    n_heads = width // hw
    grid = (b, n_heads, s // blk)
    vec_spec = pl.BlockSpec((1, DA_HEAD_DIM), lambda bi, h, i: (0, 0))
    kv_spec = pl.BlockSpec((1, s, hw), lambda bi, h, i: (bi, 0, h))
    kern = functools.partial(_diff_attn_kernel, blk=blk, scale=DA_HEAD_DIM ** -0.5, lambda_init=lambda_init)
    return pl.pallas_call(
        kern,
        out_shape=jax.ShapeDtypeStruct((b, s, width), BF16),
        grid=grid,
        in_specs=[vec_spec, vec_spec, vec_spec, vec_spec,
                  pl.BlockSpec((1, hw), lambda bi, h, i: (0, 0)),
                  pl.BlockSpec((1, blk, hw), lambda bi, h, i: (bi, i, h)),
                  kv_spec, kv_spec],
        out_specs=pl.BlockSpec((1, blk, hw), lambda bi, h, i: (bi, i, h)),
        scratch_shapes=[pltpu.VMEM((2, blk, 1), F32),
                        pltpu.VMEM((2, blk, 1), F32),
                        pltpu.VMEM((2, blk, hw), F32)],
        compiler_params=pltpu.CompilerParams(
            dimension_semantics=("parallel", "parallel", "arbitrary"),
            vmem_limit_bytes=_vmem_limit(56 * 1024 * 1024)),
        name="diff_attention",
    )(*lam_params, norm_g, qa, ka, va)


def _sb_attn_kernel(qt_ref, k_ref, vt_ref, o_ref, acc_sc, carry_sc, *, blk):
    i = pl.program_id(2)
    qt = (qt_ref[0, 0].astype(F32) * (qt_ref.shape[2] ** -0.5)).astype(BF16)
    groups = blk // SUBLANES

    acc_sc[...] = jnp.zeros(acc_sc.shape, F32)
    carry_sc[...] = jnp.zeros(carry_sc.shape, F32)
    sub = lax.broadcasted_iota(jnp.int32, (SUBLANES, blk), 0)

    def step(kb, masked):
        k = k_ref[0, 0, kb]
        vt = vt_ref[0, 0, kb]
        z = jnp.dot(k, qt, preferred_element_type=F32)
        lp = jnp.log1p(jnp.exp(-jnp.abs(z)))
        ls_pos = jnp.minimum(z, 0.0) - lp
        ls_neg = ls_pos - z
        if masked:
            r = lax.broadcasted_iota(jnp.int32, (blk, blk), 0)
            kpos = (r % SUBLANES) * groups + r // SUBLANES
            qpos = lax.broadcasted_iota(jnp.int32, (blk, blk), 1)
            strict = kpos < qpos
            ls_neg = jnp.where(strict, ls_neg, 0.0)
        tot = ls_neg[0:SUBLANES]
        for a in range(1, groups):
            tot = tot + ls_neg[a * SUBLANES:(a + 1) * SUBLANES]
        incl = tot
        for sh in (1, 2, 4):
            incl = incl + jnp.where(sub + sh < SUBLANES, pltpu.roll(incl, SUBLANES - sh, axis=0), 0.0)
        run = incl - tot + carry_sc[...]
        pieces = [None] * groups
        for a in reversed(range(groups)):
            pieces[a] = ls_pos[a * SUBLANES:(a + 1) * SUBLANES] + run
            run = run + ls_neg[a * SUBLANES:(a + 1) * SUBLANES]
        w = jnp.exp(jnp.concatenate(pieces, axis=0))
        if masked:
            w = jnp.where(strict, w, 0.0)
        acc_sc[...] += jnp.dot(vt, w.astype(BF16), preferred_element_type=F32)
        carry_sc[...] += jnp.broadcast_to(incl[0:1], (SUBLANES, blk))

    step(i, True)

    def loop_body(jj, carry):
        step(i - 1 - jj, False)
        return carry

    lax.fori_loop(0, i, loop_body, 0)
    o_ref[0] = acc_sc[...].T.astype(o_ref.dtype)


def _sb_attention(qt, kp, vtp, *, blk):
    b, h, d, s = qt.shape
    nblk = s // blk
    grid = (b, h, nblk)
    kern = functools.partial(_sb_attn_kernel, blk=blk)
    return pl.pallas_call(
        kern,
        out_shape=jax.ShapeDtypeStruct((b, s, h * d), BF16),
        grid=grid,
        in_specs=[pl.BlockSpec((1, 1, d, blk), lambda bi, hi, i: (bi, hi, 0, i)),
                  pl.BlockSpec((1, 1, nblk, blk, d), lambda bi, hi, i: (bi, hi, 0, 0, 0)),
                  pl.BlockSpec((1, 1, nblk, d, blk), lambda bi, hi, i: (bi, hi, 0, 0, 0))],
        out_specs=pl.BlockSpec((1, blk, d), lambda bi, hi, i: (bi, i, hi)),
        scratch_shapes=[pltpu.VMEM((d, blk), F32),
                        pltpu.VMEM((SUBLANES, blk), F32)],
        compiler_params=pltpu.CompilerParams(
            dimension_semantics=("parallel", "parallel", "arbitrary"),
            vmem_limit_bytes=_vmem_limit(56 * 1024 * 1024)),
        name="sb_attention",
    )(qt, kp, vtp)


def _merge_kernel(ya_ref, yb_ref, g_ref, x_ref, wa_ref, wb_ref, wo_ref, lg_ref, lb_ref, o_ref, *, alpha):
    d = x_ref.shape[1]
    pa = jnp.dot(ya_ref[...], wa_ref[...], preferred_element_type=F32)
    pb = jnp.dot(yb_ref[...], wb_ref[...], preferred_element_type=F32)
    mixed = g_ref[:, :d] * pa + g_ref[:, d:] * pb
    z = jnp.dot(mixed.astype(BF16), wo_ref[...], preferred_element_type=F32)
    o_ref[...] = _layer_norm(alpha * x_ref[...] + z, lg_ref[...], lb_ref[...])


def _const_spec(shape):
    nd = len(shape)
    return pl.BlockSpec(shape, lambda *_: (0,) * nd, pipeline_mode=pl.Buffered(1))


def _merge(ya, yb, gates, x2d, wa, wb, wo, ln_g, ln_b, *, tm, alpha):
    t, d = x2d.shape
    row = lambda width: pl.BlockSpec((tm, width), lambda i: (i, 0))
    return pl.pallas_call(
        functools.partial(_merge_kernel, alpha=alpha),
        out_shape=jax.ShapeDtypeStruct((t, d), F32),
        grid=(t // tm,),
        in_specs=[row(ya.shape[1]), row(yb.shape[1]), row(gates.shape[1]), row(d),
                  _const_spec(wa.shape), _const_spec(wb.shape), _const_spec(wo.shape),
                  _const_spec(ln_g.shape), _const_spec(ln_b.shape)],
        out_specs=row(d),
        compiler_params=pltpu.CompilerParams(
            dimension_semantics=("parallel",),
            vmem_limit_bytes=_vmem_limit(56 * 1024 * 1024)),
        name="merge_ln1",
    )(ya, yb, gates, x2d, wa, wb, wo, ln_g, ln_b)


def _cross_kernel(x_ref, wq_ref, km_ref, vm_ref, wo_ref, lg_ref, lb_ref, o_ref, *, alpha, n_heads):
    x = x_ref[...]
    d = x.shape[1]
    dh = d // n_heads
    q = jnp.dot(x.astype(BF16), wq_ref[...], preferred_element_type=F32).astype(BF16)
    km = km_ref[0]
    vm = vm_ref[0]
    outs = []
    for h in range(n_heads):
        s = lax.dot_general(q[:, h * dh:(h + 1) * dh], km[:, h * dh:(h + 1) * dh],
                            (((1,), (1,)), ((), ())), preferred_element_type=F32) * (dh ** -0.5)
        s = s - jnp.max(s, axis=-1, keepdims=True)
        p = jnp.exp(s)
        p = p / jnp.sum(p, axis=-1, keepdims=True)
        outs.append(jnp.dot(p.astype(BF16), vm[:, h * dh:(h + 1) * dh], preferred_element_type=F32))
    o = jnp.concatenate(outs, axis=-1).astype(BF16)
    c = jnp.dot(o, wo_ref[...], preferred_element_type=F32)
    o_ref[...] = _layer_norm(alpha * x + c, lg_ref[...], lb_ref[...])


def _cross_attention(x2d, wq, kmem, vmem, wo, ln_g, ln_b, *, tm, seq, alpha):
    t, d = x2d.shape
    m = kmem.shape[1]
    per_seq = seq // tm
    row = pl.BlockSpec((tm, d), lambda i: (i, 0))
    mem_spec = pl.BlockSpec((1, m, d), lambda i: (i // per_seq, 0, 0))
    return pl.pallas_call(
        functools.partial(_cross_kernel, alpha=alpha, n_heads=MEM_HEADS),
        out_shape=jax.ShapeDtypeStruct((t, d), F32),
        grid=(t // tm,),
        in_specs=[row, _const_spec(wq.shape), mem_spec, mem_spec, _const_spec(wo.shape),
                  _const_spec(ln_g.shape), _const_spec(ln_b.shape)],
        out_specs=row,
        compiler_params=pltpu.CompilerParams(
            dimension_semantics=("parallel",),
            vmem_limit_bytes=_vmem_limit(56 * 1024 * 1024)),
        name="cross_ln2",
    )(x2d, wq, kmem, vmem, wo, ln_g, ln_b)


def _swiglu_kernel(x_ref, wg_ref, wu_ref, wd_ref, lg_ref, lb_ref, o_ref, xb_sc, acc_sc, *, alpha):
    j = pl.program_id(1)

    @pl.when(j == 0)
    def _():
        xb_sc[...] = x_ref[...].astype(BF16)
        acc_sc[...] = jnp.zeros(acc_sc.shape, F32)

    xb = xb_sc[...]
    g = jnp.dot(xb, wg_ref[...], preferred_element_type=F32)
    u = jnp.dot(xb, wu_ref[...], preferred_element_type=F32)
    hmid = (g * jax.nn.sigmoid(g) * u).astype(BF16)
    acc_sc[...] += jnp.dot(hmid, wd_ref[...], preferred_element_type=F32)

    @pl.when(j == pl.num_programs(1) - 1)
    def _():
        o_ref[...] = _layer_norm(alpha * x_ref[...] + acc_sc[...], lg_ref[...], lb_ref[...])


def _swiglu(x2d, wg, wu, wd, ln_g, ln_b, *, tm, tf, alpha):
    t, d = x2d.shape
    dff = wg.shape[1]
    row = pl.BlockSpec((tm, d), lambda i, j: (i, 0))
    vec = pl.BlockSpec((1, d), lambda i, j: (0, 0))
    return pl.pallas_call(
        functools.partial(_swiglu_kernel, alpha=alpha),
        out_shape=jax.ShapeDtypeStruct((t, d), F32),
        grid=(t // tm, dff // tf),
        in_specs=[row,
                  pl.BlockSpec((d, tf), lambda i, j: (0, j)),
                  pl.BlockSpec((d, tf), lambda i, j: (0, j)),
                  pl.BlockSpec((tf, d), lambda i, j: (j, 0)),
                  vec, vec],
        out_specs=row,
        scratch_shapes=[pltpu.VMEM((tm, d), BF16), pltpu.VMEM((tm, d), F32)],
        compiler_params=pltpu.CompilerParams(
            dimension_semantics=("parallel", "arbitrary"),
            vmem_limit_bytes=_vmem_limit(56 * 1024 * 1024)),
        name="swiglu_ln3",
    )(x2d, wg, wu, wd, ln_g, ln_b)


def _rope_tables(seq):
    inv = ROPE_THETA ** (-jnp.arange(0, DA_HEAD_DIM, 2, dtype=F32) / DA_HEAD_DIM)
    ang = jnp.arange(seq, dtype=F32)[:, None] * inv[None, :]
    cos, sin = jnp.cos(ang), jnp.sin(ang)
    return jnp.concatenate([cos, cos], axis=-1), jnp.concatenate([-sin, sin], axis=-1)


def kernel(x, mem, w_in, b_gate, lam_q1, lam_k1, lam_q2, lam_k2, da_norm_g, w_proj_a, w_proj_b, w_out,
           ln1_g, ln1_b, w_mq, w_mk, w_mv, w_mo, ln2_g, ln2_b, w_gate, w_up, w_down, ln3_g, ln3_b):
    bsz, seq, d = x.shape
    depth = w_in.shape[0]
    t = bsz * seq
    da_width = DA_HEADS * 2 * DA_HEAD_DIM
    sb_width = SB_HEADS * SB_HEAD_DIM
    alpha = (2.0 * depth) ** 0.25
    blk = min(ATTN_BLK, seq)
    nblk = seq // blk
    groups = blk // SUBLANES
    rope = _rope_tables(seq)
    tm_proj = min(512, seq)
    tm_tok = min(256, seq)
    tm_ff = min(512, seq)

    x2d = x.reshape(t, d)
    for l in range(depth):
        lambda_init = 0.8 - 0.6 * math.exp(-0.3 * l)
        w_l = w_in[l].astype(BF16)
        xb = x2d.astype(BF16)
        o_qk = 2 * da_width
        o_mid = 3 * da_width + 3 * sb_width
        qk_a = _projection(xb, w_l[:, :o_qk], out_dtype=BF16, tm=tm_proj, tn=512, name="proj_rope",
                           rope=rope, seq=seq)
        mid = _projection(xb, w_l[:, o_qk:o_mid], out_dtype=BF16, tm=tm_proj, tn=512, name="proj_plain")
        gates = _projection(xb, w_l[:, o_mid:], out_dtype=F32, tm=tm_proj, tn=512, name="proj_gate",
                            bias=b_gate[l].reshape(1, -1).astype(F32))

        qa = qk_a[:, :da_width].reshape(bsz, seq, da_width)
        ka = qk_a[:, da_width:].reshape(bsz, seq, da_width)
        va = mid[:, :da_width].reshape(bsz, seq, da_width)
        qb = mid[:, da_width:da_width + sb_width].reshape(bsz, seq, SB_HEADS, SB_HEAD_DIM)
        kb = mid[:, da_width + sb_width:da_width + 2 * sb_width].reshape(bsz, nblk, SUBLANES, groups, SB_HEADS, SB_HEAD_DIM)
        vb = mid[:, da_width + 2 * sb_width:].reshape(bsz, nblk, SUBLANES, groups, SB_HEADS, SB_HEAD_DIM)

        lam_params = [p[l].reshape(1, -1).astype(F32) for p in (lam_q1, lam_k1, lam_q2, lam_k2)]
        ya = _diff_attention(qa, ka, va, lam_params, da_norm_g[l].reshape(1, -1).astype(F32),
                             blk=blk, lambda_init=lambda_init)
        qt = jnp.transpose(qb, (0, 2, 3, 1))
        kp = jnp.transpose(kb, (0, 4, 1, 3, 2, 5)).reshape(bsz, SB_HEADS, nblk, blk, SB_HEAD_DIM)
        vtp = jnp.transpose(vb, (0, 4, 1, 5, 3, 2)).reshape(bsz, SB_HEADS, nblk, SB_HEAD_DIM, blk)
        yb = _sb_attention(qt, kp, vtp, blk=blk)

        x2d = _merge(ya.reshape(t, da_width), yb.reshape(t, sb_width), gates, x2d,
                     w_proj_a[l].astype(BF16), w_proj_b[l].astype(BF16), w_out[l].astype(BF16),
                     ln1_g[l].reshape(1, -1).astype(F32), ln1_b[l].reshape(1, -1).astype(F32),
                     tm=tm_tok, alpha=alpha)
        mem2d = mem.reshape(bsz * mem.shape[1], d).astype(BF16)
        w_kv = jnp.concatenate([w_mk[l], w_mv[l]], axis=1).astype(BF16)
        kv = _projection(mem2d, w_kv, out_dtype=BF16, tm=min(256, mem2d.shape[0]), tn=512, name="proj_mem")
        kmem = kv[:, :d].reshape(bsz, mem.shape[1], d)
        vmem = kv[:, d:].reshape(bsz, mem.shape[1], d)
        x2d = _cross_attention(x2d, w_mq[l].astype(BF16), kmem, vmem, w_mo[l].astype(BF16),
                               ln2_g[l].reshape(1, -1).astype(F32), ln2_b[l].reshape(1, -1).astype(F32),
                               tm=tm_tok, seq=seq, alpha=alpha)
        x2d = _swiglu(x2d, w_gate[l].astype(BF16), w_up[l].astype(BF16), w_down[l].astype(BF16),
                      ln3_g[l].reshape(1, -1).astype(F32), ln3_b[l].reshape(1, -1).astype(F32),
                      tm=tm_ff, tf=512, alpha=alpha)
    return x2d.reshape(bsz, seq, d)
```

```python
import functools
import math

import jax
import jax.numpy as jnp
from jax import lax
from jax.experimental import pallas as pl
from jax.experimental.pallas import tpu as pltpu

F32 = jnp.float32
BF16 = jnp.bfloat16

CHUNK = 64
ROPE_THETA = 10000.0
LN_EPS = 1e-5
DA_HEADS = 4
DA_HEAD_DIM = 128
SB_HEADS = 4
SB_HEAD_DIM = 256
MEM_HEADS = 4
LOG2E = math.log2(math.e)

LANES = 128
SUBLANES = 8
VMEM_BYTES = 64 * 1024 * 1024

ATTN_BLK = 512
HEAD_COLS = 256


def _vmem_limit(nbytes):
    return int(min(nbytes, VMEM_BYTES - 6 * 1024 * 1024))


def _layer_norm(r, g, b):
    mu = jnp.mean(r, axis=-1, keepdims=True)
    d = r - mu
    var = jnp.mean(d * d, axis=-1, keepdims=True)
    return d * lax.rsqrt(var + LN_EPS) * g + b


def _const_spec(shape):
    nd = len(shape)
    return pl.BlockSpec(shape, lambda *_: (0,) * nd, pipeline_mode=pl.Buffered(1))


def _key_of_row(r, blk):
    return (r % SUBLANES) * (blk // SUBLANES) + r // SUBLANES


def _proj_kernel(*refs, rope, scale, perm, layout, gate, blk):
    it = iter(refs)
    x_ref, w_ref = next(it), next(it)
    cos_ref, sin_ref = (next(it), next(it)) if rope else (None, None)
    b_ref = next(it) if gate else None
    o_ref = next(it)
    tm = x_ref.shape[0]
    x = x_ref[...]
    if perm:
        r = lax.broadcasted_iota(jnp.int32, (blk, blk), 0)
        c = lax.broadcasted_iota(jnp.int32, (blk, blk), 1)
        pmat = (c == _key_of_row(r, blk)).astype(BF16)
        x = jnp.concatenate(
            [jnp.dot(pmat, x[s * blk:(s + 1) * blk], preferred_element_type=F32).astype(BF16)
             for s in range(tm // blk)], axis=0)
    acc = jnp.dot(x, w_ref[...], preferred_element_type=F32)
    n = acc.shape[1]
    if rope:
        cos = cos_ref[...]
        sin = sin_ref[...]
        heads = []
        for h in range(n // DA_HEAD_DIM):
            t = acc[:, h * DA_HEAD_DIM:(h + 1) * DA_HEAD_DIM]
            heads.append(t * cos + pltpu.roll(t, DA_HEAD_DIM // 2, axis=1) * sin)
        acc = jnp.concatenate(heads, axis=1)
    if scale is not None:
        acc = acc * scale
    if gate:
        acc = jax.nn.sigmoid(acc + b_ref[...])
    if layout == "rows":
        o_ref[...] = acc.astype(o_ref.dtype)
    elif layout == "cols":
        o_ref[0] = acc.T.astype(o_ref.dtype)
    else:
        for h in range(n // HEAD_COLS):
            for s in range(tm // blk):
                tile = acc[s * blk:(s + 1) * blk, h * HEAD_COLS:(h + 1) * HEAD_COLS]
                o_ref[0, h, s] = tile.T.astype(o_ref.dtype)


def _projection(x2d, w, *, name, bsz, seq, blk, tm, tn, out_dtype=BF16, rope=None, scale=None, perm=False,
                layout="rows", bias=None):
    t, d = x2d.shape
    n = w.shape[1]
    per_seq = seq // tm
    grid = (t // tm, n // tn)
    in_specs = [pl.BlockSpec((tm, d), lambda i, j: (i, 0)),
                pl.BlockSpec((d, tn), lambda i, j: (0, j))]
    args = [x2d, w]
    if rope is not None:
        tab_spec = pl.BlockSpec((tm, DA_HEAD_DIM), lambda i, j: (i % per_seq, 0))
        in_specs += [tab_spec, tab_spec]
        args += list(rope)
    if bias is not None:
        in_specs.append(pl.BlockSpec((1, tn), lambda i, j: (0, j)))
        args.append(bias)
    if layout == "rows":
        out_shape = jax.ShapeDtypeStruct((t, n), out_dtype)
        out_spec = pl.BlockSpec((tm, tn), lambda i, j: (i, j))
    elif layout == "cols":
        out_shape = jax.ShapeDtypeStruct((bsz, n, seq), out_dtype)
        out_spec = pl.BlockSpec((1, tn, tm), lambda i, j: (i // per_seq, j, i % per_seq))
    else:
        out_shape = jax.ShapeDtypeStruct((bsz, n // HEAD_COLS, seq // blk, HEAD_COLS, blk), out_dtype)
        out_spec = pl.BlockSpec((1, tn // HEAD_COLS, tm // blk, HEAD_COLS, blk),
                                lambda i, j: (i // per_seq, j, i % per_seq, 0, 0))
    kern = functools.partial(_proj_kernel, rope=rope is not None, scale=scale, perm=perm, layout=layout,
                             gate=bias is not None, blk=blk)
    return pl.pallas_call(
        kern,
        out_shape=out_shape,
        grid=grid,
        in_specs=in_specs,
        out_specs=out_spec,
        compiler_params=pltpu.CompilerParams(
            dimension_semantics=("parallel", "arbitrary"),
            vmem_limit_bytes=_vmem_limit(56 * 1024 * 1024)),
        name=name,
    )(*args)


def _sublane_allmax(v):
    for sh in (4, 2, 1):
        v = jnp.maximum(v, pltpu.roll(v, sh, axis=0))
    return v


def _diff_attn_kernel(lq1_ref, lk1_ref, lq2_ref, lk2_ref, g_ref, qt_ref, k_ref, vt_ref, o_ref,
                      m_sc, l_sc, acc_sc, s_sc, p_sc, a_sc, *, blk, lambda_init):
    i = pl.program_id(2)
    dh = DA_HEAD_DIM
    groups = blk // SUBLANES
    qt = qt_ref[0]

    m_sc[...] = jnp.full(m_sc.shape, -jnp.inf, F32)
    l_sc[...] = jnp.zeros(l_sc.shape, F32)
    acc_sc[...] = jnp.zeros(acc_sc.shape, F32)

    def scores(j):
        start = pl.multiple_of(j * blk, blk)
        k = k_ref[0, pl.ds(start, blk), :]
        return [jnp.dot(k[:, c * dh:(c + 1) * dh], qt[c * dh:(c + 1) * dh, :], preferred_element_type=F32)
                for c in range(2)]

    def softmax(s_pair, masked):
        if masked:
            key = lax.broadcasted_iota(jnp.int32, (blk, blk), 0)
            qry = lax.broadcasted_iota(jnp.int32, (blk, blk), 1)
            allowed = (key // CHUNK) <= (qry // CHUNK)
        ps, alphas = [], []
        for c in range(2):
            s = jnp.where(allowed, s_pair[c], -jnp.inf) if masked else s_pair[c]
            s3 = s.reshape(groups, SUBLANES, blk)
            m_prev = m_sc[c]
            m_new = jnp.maximum(m_prev, _sublane_allmax(jnp.max(s3, axis=0)))
            alpha = jnp.exp2(m_prev - m_new)
            p3 = jnp.exp2(s3 - m_new[None])
            l_sc[c] = alpha * l_sc[c] + jnp.sum(p3, axis=0)
            m_sc[c] = m_new
            ps.append(p3.reshape(blk, blk).astype(BF16))
            alphas.append(alpha)
        return ps, alphas

    def put(ref, slot, pair):
        for c in range(2):
            ref[slot, c] = pair[c]

    def softmax_to(slot, s_pair, masked):
        ps, alphas = softmax(s_pair, masked)
        put(p_sc, slot, ps)
        put(a_sc, slot, alphas)

    def accumulate(j, slot):
        vt = vt_ref[0, 0, j]
        for c in range(2):
            pv = jnp.dot(vt, p_sc[slot, c], preferred_element_type=F32)
            acc_sc[c] = acc_sc[c] * a_sc[slot, c][0:1] + pv

    softmax_to(0, scores(i), True)
    put(s_sc, 0, scores(jnp.maximum(i - 1, 0)))

    def trip(kb, cur):
        put(s_sc, 1 - cur, scores(jnp.maximum(kb - 1, 0)))
        softmax_to(1 - cur, [s_sc[cur, c] for c in range(2)], False)
        accumulate(kb + 1, cur)

    def loop_body(t, carry):
        for cur in range(2):
            pl.when(t % 2 == cur)(functools.partial(trip, i - 1 - t, cur))
        return carry

    lax.fori_loop(0, i, loop_body, 0)
    accumulate(0, i % 2)

    lam = (jnp.exp(jnp.sum(lq1_ref[...] * lk1_ref[...], axis=-1, keepdims=True))
           - jnp.exp(jnp.sum(lq2_ref[...] * lk2_ref[...], axis=-1, keepdims=True))
           + lambda_init)
    l0 = jnp.sum(l_sc[0], axis=0, keepdims=True)
    l1 = jnp.sum(l_sc[1], axis=0, keepdims=True)
    o = (acc_sc[0] / l0 - lam * (acc_sc[1] / l1)).T
    o = o * lax.rsqrt(jnp.mean(o * o, axis=-1, keepdims=True) + LN_EPS)
    o = o * g_ref[...] * (1.0 - lambda_init)
    o_ref[0] = o.astype(o_ref.dtype)


def _diff_attention(qt, ka, vt, lam_params, norm_g, *, blk, lambda_init):
    b, width, s = qt.shape
    hw = 2 * DA_HEAD_DIM
    n_heads = width // hw
    nblk = s // blk
    vec_spec = pl.BlockSpec((1, DA_HEAD_DIM), lambda bi, h, i: (0, 0))
    kern = functools.partial(_diff_attn_kernel, blk=blk, lambda_init=lambda_init)
    return pl.pallas_call(
        kern,
        out_shape=jax.ShapeDtypeStruct((b, s, width), BF16),
        grid=(b, n_heads, nblk),
        in_specs=[vec_spec, vec_spec, vec_spec, vec_spec,
                  pl.BlockSpec((1, hw), lambda bi, h, i: (0, 0)),
                  pl.BlockSpec((1, hw, blk), lambda bi, h, i: (bi, h, i)),
                  pl.BlockSpec((1, s, hw), lambda bi, h, i: (bi, 0, h)),
                  pl.BlockSpec((1, 1, nblk, hw, blk), lambda bi, h, i: (bi, h, 0, 0, 0))],
        out_specs=pl.BlockSpec((1, blk, hw), lambda bi, h, i: (bi, i, h)),
        scratch_shapes=[pltpu.VMEM((2, SUBLANES, blk), F32),
                        pltpu.VMEM((2, SUBLANES, blk), F32),
                        pltpu.VMEM((2, hw, blk), F32),
                        pltpu.VMEM((2, 2, blk, blk), F32),
                        pltpu.VMEM((2, 2, blk, blk), BF16),
                        pltpu.VMEM((2, 2, SUBLANES, blk), F32)],
        compiler_params=pltpu.CompilerParams(
            dimension_semantics=("parallel", "parallel", "arbitrary"),
            vmem_limit_bytes=_vmem_limit(56 * 1024 * 1024)),
        name="diff_attention",
    )(*lam_params, norm_g, qt, ka, vt)


def _sb_attn_kernel(qt_ref, k_ref, vt_ref, o_ref, acc_sc, carry_sc, z_sc, w_sc, piece_sc, *, blk):
    i = pl.program_id(2)
    qt = qt_ref[0]
    groups = blk // SUBLANES

    acc_sc[...] = jnp.zeros(acc_sc.shape, F32)
    carry_sc[...] = jnp.zeros(carry_sc.shape, F32)
    sub = lax.broadcasted_iota(jnp.int32, (SUBLANES, blk), 0)

    def scores(kb):
        start = pl.multiple_of(kb * blk, blk)
        return jnp.dot(k_ref[0, pl.ds(start, blk), :], qt, preferred_element_type=F32)

    def weights(src, dst, masked):
        qpos = lax.broadcasted_iota(jnp.int32, (SUBLANES, blk), 1)
        run = jnp.zeros((SUBLANES, blk), F32)
        for a in reversed(range(groups)):
            rows = pl.ds(a * SUBLANES, SUBLANES)
            z = z_sc[src, rows, :]
            neg_abs = pltpu.bitcast(pltpu.bitcast(z, jnp.uint32) | jnp.uint32(0x80000000), F32)
            ls_pos = jnp.minimum(z, 0.0) - jnp.log(1.0 + jnp.exp2(neg_abs)) * LOG2E
            piece = ls_pos + run
            if masked:
                run = run + jnp.where(sub * groups + a < qpos, ls_pos - z, 0.0)
            else:
                run = piece - z
            piece_sc[rows, :] = piece
        incl = run
        for sh in (1, 2, 4):
            incl = incl + jnp.where(sub + sh < SUBLANES, pltpu.roll(incl, SUBLANES - sh, axis=0), 0.0)
        offs = incl - run + carry_sc[...]
        w = jnp.exp2(piece_sc[...].reshape(groups, SUBLANES, blk) + offs[None]).reshape(blk, blk)
        if masked:
            r = lax.broadcasted_iota(jnp.int32, (blk, blk), 0)
            w = jnp.where(_key_of_row(r, blk) < lax.broadcasted_iota(jnp.int32, (blk, blk), 1), w, 0.0)
        w_sc[dst] = w.astype(BF16)
        carry_sc[...] += jnp.broadcast_to(incl[0:1], (SUBLANES, blk))

    def accumulate(kb, slot):
        acc_sc[...] += jnp.dot(vt_ref[0, 0, kb], w_sc[slot], preferred_element_type=F32)

    z_sc[1] = scores(i)
    weights(1, 0, True)
    z_sc[0] = scores(jnp.maximum(i - 1, 0))

    def trip(kb, cur):
        z_sc[1 - cur] = scores(jnp.maximum(kb - 1, 0))
        weights(cur, 1 - cur, False)
        accumulate(kb + 1, cur)

    def loop_body(t, carry):
        for cur in range(2):
            pl.when(t % 2 == cur)(functools.partial(trip, i - 1 - t, cur))
        return carry

    lax.fori_loop(0, i, loop_body, 0)
    accumulate(0, i % 2)
    o_ref[0] = acc_sc[...].T.astype(o_ref.dtype)


def _sb_attention(qt, kp, vtp, *, blk):
    b, width, s = qt.shape
    d = SB_HEAD_DIM
    h = width // d
    nblk = s // blk
    kern = functools.partial(_sb_attn_kernel, blk=blk)
    return pl.pallas_call(
        kern,
        out_shape=jax.ShapeDtypeStruct((b, s, width), BF16),
        grid=(b, h, nblk),
        in_specs=[pl.BlockSpec((1, d, blk), lambda bi, hi, i: (bi, hi, i)),
                  pl.BlockSpec((1, s, d), lambda bi, hi, i: (bi, 0, hi)),
                  pl.BlockSpec((1, 1, nblk, d, blk), lambda bi, hi, i: (bi, hi, 0, 0, 0))],
        out_specs=pl.BlockSpec((1, blk, d), lambda bi, hi, i: (bi, i, hi)),
        scratch_shapes=[pltpu.VMEM((d, blk), F32),
                        pltpu.VMEM((SUBLANES, blk), F32),
                        pltpu.VMEM((2, blk, blk), F32),
                        pltpu.VMEM((2, blk, blk), BF16),
                        pltpu.VMEM((blk, blk), F32)],
        compiler_params=pltpu.CompilerParams(
            dimension_semantics=("parallel", "parallel", "arbitrary"),
            vmem_limit_bytes=_vmem_limit(56 * 1024 * 1024)),
        name="sb_attention",
    )(qt, kp, vtp)


def _merge_kernel(ya_ref, yb_ref, g_ref, x_ref, wa_ref, wb_ref, wo_ref, lg_ref, lb_ref, o_ref, *, alpha):
    d = x_ref.shape[1]
    pa = jnp.dot(ya_ref[...], wa_ref[...], preferred_element_type=F32)
    pb = jnp.dot(yb_ref[...], wb_ref[...], preferred_element_type=F32)
    mixed = g_ref[:, :d] * pa + g_ref[:, d:] * pb
    z = jnp.dot(mixed.astype(BF16), wo_ref[...], preferred_element_type=F32)
    o_ref[...] = _layer_norm(alpha * x_ref[...] + z, lg_ref[...], lb_ref[...])


def _merge(ya, yb, gates, x2d, wa, wb, wo, ln_g, ln_b, *, tm, alpha):
    t, d = x2d.shape
    row = lambda width: pl.BlockSpec((tm, width), lambda i: (i, 0))
    return pl.pallas_call(
        functools.partial(_merge_kernel, alpha=alpha),
        out_shape=jax.ShapeDtypeStruct((t, d), F32),
        grid=(t // tm,),
        in_specs=[row(ya.shape[1]), row(yb.shape[1]), row(gates.shape[1]), row(d),
                  _const_spec(wa.shape), _const_spec(wb.shape), _const_spec(wo.shape),
                  _const_spec(ln_g.shape), _const_spec(ln_b.shape)],
        out_specs=row(d),
        compiler_params=pltpu.CompilerParams(
            dimension_semantics=("parallel",),
            vmem_limit_bytes=_vmem_limit(56 * 1024 * 1024)),
        name="merge_ln1",
    )(ya, yb, gates, x2d, wa, wb, wo, ln_g, ln_b)


def _cross_kernel(x_ref, wq_ref, km_ref, vm_ref, wo_ref, lg_ref, lb_ref, o_ref, *, alpha, n_heads):
    x = x_ref[...]
    d = x.shape[1]
    dh = d // n_heads
    q = jnp.dot(x.astype(BF16), wq_ref[...], preferred_element_type=F32).astype(BF16)
    km = km_ref[0]
    vm = vm_ref[0]
    outs = []
    for h in range(n_heads):
        s = lax.dot_general(q[:, h * dh:(h + 1) * dh], km[:, h * dh:(h + 1) * dh],
                            (((1,), (1,)), ((), ())), preferred_element_type=F32) * (dh ** -0.5)
        s = s - jnp.max(s, axis=-1, keepdims=True)
        p = jnp.exp(s)
        p = p / jnp.sum(p, axis=-1, keepdims=True)
        outs.append(jnp.dot(p.astype(BF16), vm[:, h * dh:(h + 1) * dh], preferred_element_type=F32))
    o = jnp.concatenate(outs, axis=-1).astype(BF16)
    c = jnp.dot(o, wo_ref[...], preferred_element_type=F32)
    o_ref[...] = _layer_norm(alpha * x + c, lg_ref[...], lb_ref[...])


def _cross_attention(x2d, wq, kmem, vmem, wo, ln_g, ln_b, *, tm, seq, alpha):
    t, d = x2d.shape
    m = kmem.shape[1]
    per_seq = seq // tm
    row = pl.BlockSpec((tm, d), lambda i: (i, 0))
    mem_spec = pl.BlockSpec((1, m, d), lambda i: (i // per_seq, 0, 0))
    return pl.pallas_call(
        functools.partial(_cross_kernel, alpha=alpha, n_heads=MEM_HEADS),
        out_shape=jax.ShapeDtypeStruct((t, d), F32),
        grid=(t // tm,),
        in_specs=[row, _const_spec(wq.shape), mem_spec, mem_spec, _const_spec(wo.shape),
                  _const_spec(ln_g.shape), _const_spec(ln_b.shape)],
        out_specs=row,
        compiler_params=pltpu.CompilerParams(
            dimension_semantics=("parallel",),
            vmem_limit_bytes=_vmem_limit(56 * 1024 * 1024)),
        name="cross_ln2",
    )(x2d, wq, kmem, vmem, wo, ln_g, ln_b)


def _swiglu_kernel(x_ref, wg_ref, wu_ref, wd_ref, lg_ref, lb_ref, o_ref, xb_sc, acc_sc, *, alpha):
    j = pl.program_id(1)

    @pl.when(j == 0)
    def _():
        xb_sc[...] = x_ref[...].astype(BF16)
        acc_sc[...] = jnp.zeros(acc_sc.shape, F32)

    xb = xb_sc[...]
    g = jnp.dot(xb, wg_ref[...], preferred_element_type=F32)
    u = jnp.dot(xb, wu_ref[...], preferred_element_type=F32)
    hmid = (g * jax.nn.sigmoid(g) * u).astype(BF16)
    acc_sc[...] += jnp.dot(hmid, wd_ref[...], preferred_element_type=F32)

    @pl.when(j == pl.num_programs(1) - 1)
    def _():
        o_ref[...] = _layer_norm(alpha * x_ref[...] + acc_sc[...], lg_ref[...], lb_ref[...])


def _swiglu(x2d, wg, wu, wd, ln_g, ln_b, *, tm, tf, alpha):
    t, d = x2d.shape
    dff = wg.shape[1]
    row = pl.BlockSpec((tm, d), lambda i, j: (i, 0))
    vec = pl.BlockSpec((1, d), lambda i, j: (0, 0))
    return pl.pallas_call(
        functools.partial(_swiglu_kernel, alpha=alpha),
        out_shape=jax.ShapeDtypeStruct((t, d), F32),
        grid=(t // tm, dff // tf),
        in_specs=[row,
                  pl.BlockSpec((d, tf), lambda i, j: (0, j)),
                  pl.BlockSpec((d, tf), lambda i, j: (0, j)),
                  pl.BlockSpec((tf, d), lambda i, j: (j, 0)),
                  vec, vec],
        out_specs=row,
        scratch_shapes=[pltpu.VMEM((tm, d), BF16), pltpu.VMEM((tm, d), F32)],
        compiler_params=pltpu.CompilerParams(
            dimension_semantics=("parallel", "arbitrary"),
            vmem_limit_bytes=_vmem_limit(56 * 1024 * 1024)),
        name="swiglu_ln3",
    )(x2d, wg, wu, wd, ln_g, ln_b)


def _rope_tables(seq):
    inv = ROPE_THETA ** (-jnp.arange(0, DA_HEAD_DIM, 2, dtype=F32) / DA_HEAD_DIM)
    ang = jnp.arange(seq, dtype=F32)[:, None] * inv[None, :]
    cos, sin = jnp.cos(ang), jnp.sin(ang)
    return jnp.concatenate([cos, cos], axis=-1), jnp.concatenate([-sin, sin], axis=-1)


def kernel(x, mem, w_in, b_gate, lam_q1, lam_k1, lam_q2, lam_k2, da_norm_g, w_proj_a, w_proj_b, w_out,
           ln1_g, ln1_b, w_mq, w_mk, w_mv, w_mo, ln2_g, ln2_b, w_gate, w_up, w_down, ln3_g, ln3_b):
    bsz, seq, d = x.shape
    depth = w_in.shape[0]
    t = bsz * seq
    da_width = DA_HEADS * 2 * DA_HEAD_DIM
    sb_width = SB_HEADS * SB_HEAD_DIM
    alpha = (2.0 * depth) ** 0.25
    blk = min(ATTN_BLK, seq)
    rope = _rope_tables(seq)
    tm_proj = min(1024, seq)
    tm_tok = min(256, seq)
    tm_ff = min(512, seq)
    row_vec = lambda p: p.reshape(1, -1).astype(F32)

    x2d = x.reshape(t, d)
    for l in range(depth):
        lambda_init = 0.8 - 0.6 * math.exp(-0.3 * l)
        w_l = w_in[l].astype(BF16)
        xb = x2d.astype(BF16)
        proj = functools.partial(_projection, xb, bsz=bsz, seq=seq, blk=blk, tm=tm_proj, tn=1024)
        bounds = [0, da_width, 2 * da_width, 3 * da_width, 3 * da_width + sb_width,
                  3 * da_width + 2 * sb_width, 3 * da_width + 3 * sb_width, w_l.shape[1]]
        w_qa, w_ka, w_va, w_qb, w_kb, w_vb, w_g = [w_l[:, lo:hi] for lo, hi in zip(bounds[:-1], bounds[1:])]
        qa_t = proj(w_qa, name="proj_qa", rope=rope, scale=DA_HEAD_DIM ** -0.5 * LOG2E, layout="cols")
        ka = proj(w_ka, name="proj_ka", rope=rope)
        va_t = proj(w_va, name="proj_va", layout="blocked")
        qb_t = proj(w_qb, name="proj_qb", scale=SB_HEAD_DIM ** -0.5 * LOG2E, layout="cols")
        kb_p = proj(w_kb, name="proj_kb", perm=True)
        vb_t = proj(w_vb, name="proj_vb", perm=True, layout="blocked")
        gates = proj(w_g, name="proj_gate", out_dtype=F32, bias=row_vec(b_gate[l]))

        ya = _diff_attention(qa_t, ka.reshape(bsz, seq, da_width), va_t,
                             [row_vec(p[l]) for p in (lam_q1, lam_k1, lam_q2, lam_k2)],
                             row_vec(da_norm_g[l]), blk=blk, lambda_init=lambda_init)
        yb = _sb_attention(qb_t, kb_p.reshape(bsz, seq, sb_width), vb_t, blk=blk)

        x2d = _merge(ya.reshape(t, da_width), yb.reshape(t, sb_width), gates, x2d,
                     w_proj_a[l].astype(BF16), w_proj_b[l].astype(BF16), w_out[l].astype(BF16),
                     row_vec(ln1_g[l]), row_vec(ln1_b[l]), tm=tm_tok, alpha=alpha)
        n_mem = mem.shape[1]
        mem2d = mem.reshape(bsz * n_mem, d).astype(BF16)
        w_kv = jnp.concatenate([w_mk[l], w_mv[l]], axis=1).astype(BF16)
        kv = _projection(mem2d, w_kv, name="proj_mem", bsz=bsz, seq=n_mem, blk=blk,
                         tm=min(256, n_mem), tn=1024)
        kmem = kv[:, :d].reshape(bsz, n_mem, d)
        vmem = kv[:, d:].reshape(bsz, n_mem, d)
        x2d = _cross_attention(x2d, w_mq[l].astype(BF16), kmem, vmem, w_mo[l].astype(BF16),
                               row_vec(ln2_g[l]), row_vec(ln2_b[l]), tm=tm_tok, seq=seq, alpha=alpha)
        x2d = _swiglu(x2d, w_gate[l].astype(BF16), w_up[l].astype(BF16), w_down[l].astype(BF16),
                      row_vec(ln3_g[l]), row_vec(ln3_b[l]), tm=tm_ff, tf=512, alpha=alpha)
    return x2d.reshape(bsz, seq, d)
```

```python
import functools
import math

import jax
import jax.numpy as jnp
from jax import lax
from jax.experimental import pallas as pl
from jax.experimental.pallas import tpu as pltpu

F32 = jnp.float32
BF16 = jnp.bfloat16

CHUNK = 64
ROPE_THETA = 10000.0
LN_EPS = 1e-5
DA_HEADS = 4
DA_HEAD_DIM = 128
SB_HEADS = 4
SB_HEAD_DIM = 256
MEM_HEADS = 4
LOG2E = math.log2(math.e)

LANES = 128
SUBLANES = 8
VMEM_BYTES = 64 * 1024 * 1024

ATTN_BLK = 512
HEAD_COLS = 256


def _vmem_limit(nbytes):
    return int(min(nbytes, VMEM_BYTES - 6 * 1024 * 1024))


def _layer_norm(r, g, b):
    mu = jnp.mean(r, axis=-1, keepdims=True)
    d = r - mu
    var = jnp.mean(d * d, axis=-1, keepdims=True)
    return d * lax.rsqrt(var + LN_EPS) * g + b


def _const_spec(shape):
    nd = len(shape)
    return pl.BlockSpec(shape, lambda *_: (0,) * nd, pipeline_mode=pl.Buffered(1))


def _key_of_row(r, blk):
    return (r % SUBLANES) * (blk // SUBLANES) + r // SUBLANES


def _proj_kernel(*refs, rope, scale, perm, layout, gate, emit_x, blk):
    it = iter(refs)
    x_ref, w_ref = next(it), next(it)
    cos_ref, sin_ref = (next(it), next(it)) if rope else (None, None)
    b_ref = next(it) if gate else None
    o_ref = next(it)
    tm = x_ref.shape[0]
    x = x_ref[...].astype(BF16)
    if emit_x:
        next(it)[...] = x
    if perm:
        r = lax.broadcasted_iota(jnp.int32, (blk, blk), 0)
        c = lax.broadcasted_iota(jnp.int32, (blk, blk), 1)
        pmat = (c == _key_of_row(r, blk)).astype(BF16)
        x = jnp.concatenate(
            [jnp.dot(pmat, x[s * blk:(s + 1) * blk], preferred_element_type=F32).astype(BF16)
             for s in range(tm // blk)], axis=0)
    acc = jnp.dot(x, w_ref[...], preferred_element_type=F32)
    n = acc.shape[1]
    if rope:
        cos = cos_ref[...]
        sin = sin_ref[...]
        heads = []
        for h in range(n // DA_HEAD_DIM):
            t = acc[:, h * DA_HEAD_DIM:(h + 1) * DA_HEAD_DIM]
            heads.append(t * cos + pltpu.roll(t, DA_HEAD_DIM // 2, axis=1) * sin)
        acc = jnp.concatenate(heads, axis=1)
    if scale is not None:
        acc = acc * scale
    if gate:
        acc = jax.nn.sigmoid(acc + b_ref[...])
    if layout == "rows":
        o_ref[...] = acc.astype(o_ref.dtype)
    elif layout == "cols":
        o_ref[0] = acc.T.astype(o_ref.dtype)
    else:
        for h in range(n // HEAD_COLS):
            for s in range(tm // blk):
                tile = acc[s * blk:(s + 1) * blk, h * HEAD_COLS:(h + 1) * HEAD_COLS]
                o_ref[0, h, s] = tile.T.astype(o_ref.dtype)


def _projection(x2d, w, *, name, bsz, seq, blk, tm, tn, out_dtype=BF16, rope=None, scale=None, perm=False,
                layout="rows", bias=None, emit_x=False):
    t, d = x2d.shape
    n = w.shape[1]
    per_seq = seq // tm
    grid = (t // tm, n // tn)
    in_specs = [pl.BlockSpec((tm, d), lambda i, j: (i, 0)),
                pl.BlockSpec((d, tn), lambda i, j: (0, j))]
    args = [x2d, w]
    if rope is not None:
        tab_spec = pl.BlockSpec((tm, DA_HEAD_DIM), lambda i, j: (i % per_seq, 0))
        in_specs += [tab_spec, tab_spec]
        args += list(rope)
    if bias is not None:
        in_specs.append(pl.BlockSpec((1, tn), lambda i, j: (0, j)))
        args.append(bias)
    if layout == "rows":
        out_shape = jax.ShapeDtypeStruct((t, n), out_dtype)
        out_spec = pl.BlockSpec((tm, tn), lambda i, j: (i, j))
    elif layout == "cols":
        out_shape = jax.ShapeDtypeStruct((bsz, n, seq), out_dtype)
        out_spec = pl.BlockSpec((1, tn, tm), lambda i, j: (i // per_seq, j, i % per_seq))
    else:
        out_shape = jax.ShapeDtypeStruct((bsz, n // HEAD_COLS, seq // blk, HEAD_COLS, blk), out_dtype)
        out_spec = pl.BlockSpec((1, tn // HEAD_COLS, tm // blk, HEAD_COLS, blk),
                                lambda i, j: (i // per_seq, j, i % per_seq, 0, 0))
    if emit_x:
        assert n == tn
        out_shape = (out_shape, jax.ShapeDtypeStruct((t, d), BF16))
        out_spec = (out_spec, pl.BlockSpec((tm, d), lambda i, j: (i, 0)))
    kern = functools.partial(_proj_kernel, rope=rope is not None, scale=scale, perm=perm, layout=layout,
                             gate=bias is not None, emit_x=emit_x, blk=blk)
    return pl.pallas_call(
        kern,
        out_shape=out_shape,
        grid=grid,
        in_specs=in_specs,
        out_specs=out_spec,
        compiler_params=pltpu.CompilerParams(
            dimension_semantics=("parallel", "arbitrary"),
            vmem_limit_bytes=_vmem_limit(56 * 1024 * 1024)),
        name=name,
    )(*args)


def _sublane_allmax(v):
    for sh in (4, 2, 1):
        v = jnp.maximum(v, pltpu.roll(v, sh, axis=0))
    return v


def _diff_parts(i, qt_ref, k_ref, vt_ref, m_sc, l_sc, acc_sc, s_sc, p_sc, a_sc, *, blk):
    dh = DA_HEAD_DIM
    groups = blk // SUBLANES

    def scores(j):
        start = pl.multiple_of(j * blk, blk)
        k = k_ref[0, pl.ds(start, blk), :]
        qt = qt_ref[0]
        return [jnp.dot(k[:, c * dh:(c + 1) * dh], qt[c * dh:(c + 1) * dh, :], preferred_element_type=F32)
                for c in range(2)]

    def put(ref, slot, pair):
        for c in range(2):
            ref[slot, c] = pair[c]

    def softmax(src, dst, masked):
        if masked:
            key = lax.broadcasted_iota(jnp.int32, (blk, blk), 0)
            qry = lax.broadcasted_iota(jnp.int32, (blk, blk), 1)
            allowed = (key // CHUNK) <= (qry // CHUNK)
        for c in range(2):
            s = s_sc[src, c]
            if masked:
                s = jnp.where(allowed, s, -jnp.inf)
            s3 = s.reshape(groups, SUBLANES, blk)
            m_prev = m_sc[c]
            m_new = jnp.maximum(m_prev, _sublane_allmax(jnp.max(s3, axis=0)))
            alpha = jnp.exp2(m_prev - m_new)
            p3 = jnp.exp2(s3 - m_new[None])
            l_sc[c] = alpha * l_sc[c] + jnp.sum(p3, axis=0)
            m_sc[c] = m_new
            p_sc[dst, c] = p3.reshape(blk, blk).astype(BF16)
            a_sc[dst, c] = alpha

    def accumulate(j, slot):
        vt = vt_ref[0, 0, j]
        for c in range(2):
            pv = jnp.dot(vt, p_sc[slot, c], preferred_element_type=F32)
            acc_sc[c] = acc_sc[c] * a_sc[slot, c][0:1] + pv

    def start():
        m_sc[...] = jnp.full(m_sc.shape, -jnp.inf, F32)
        l_sc[...] = jnp.zeros(l_sc.shape, F32)
        acc_sc[...] = jnp.zeros(acc_sc.shape, F32)
        put(s_sc, 1, scores(i))
        softmax(1, 0, True)
        put(s_sc, 0, scores(jnp.maximum(i - 1, 0)))

    def trip(kb, cur):
        return (lambda: put(s_sc, 1 - cur, scores(jnp.maximum(kb - 1, 0))),
                lambda: softmax(cur, 1 - cur, False),
                lambda: accumulate(kb + 1, cur))

    def finish(lq1_ref, lk1_ref, lq2_ref, lk2_ref, g_ref, o_ref, lambda_init):
        accumulate(0, i % 2)
        lam = (jnp.exp(jnp.sum(lq1_ref[...] * lk1_ref[...], axis=-1, keepdims=True))
               - jnp.exp(jnp.sum(lq2_ref[...] * lk2_ref[...], axis=-1, keepdims=True))
               + lambda_init)
        l0 = jnp.sum(l_sc[0], axis=0, keepdims=True)
        l1 = jnp.sum(l_sc[1], axis=0, keepdims=True)
        o = (acc_sc[0] / l0 - lam * (acc_sc[1] / l1)).T
        o = o * lax.rsqrt(jnp.mean(o * o, axis=-1, keepdims=True) + LN_EPS)
        o = o * g_ref[...] * (1.0 - lambda_init)
        o_ref[0] = o.astype(o_ref.dtype)

    return start, trip, finish


def _sb_parts(i, qt_ref, k_ref, vt_ref, acc_sc, carry_sc, z_sc, w_sc, piece_sc, *, blk):
    groups = blk // SUBLANES

    def scores(kb):
        start = pl.multiple_of(kb * blk, blk)
        return jnp.dot(k_ref[0, pl.ds(start, blk), :], qt_ref[0], preferred_element_type=F32)

    def weights(src, dst, masked):
        sub = lax.broadcasted_iota(jnp.int32, (SUBLANES, blk), 0)
        qpos = lax.broadcasted_iota(jnp.int32, (SUBLANES, blk), 1)
        run = jnp.zeros((SUBLANES, blk), F32)
        for a in reversed(range(groups)):
            rows = pl.ds(a * SUBLANES, SUBLANES)
            z = z_sc[src, rows, :]
            neg_abs = pltpu.bitcast(pltpu.bitcast(z, jnp.uint32) | jnp.uint32(0x80000000), F32)
            ls_pos = jnp.minimum(z, 0.0) - jnp.log(1.0 + jnp.exp2(neg_abs)) * LOG2E
            piece = ls_pos + run
            if masked:
                run = run + jnp.where(sub * groups + a < qpos, ls_pos - z, 0.0)
            else:
                run = piece - z
            piece_sc[rows, :] = piece
        incl = run
        for sh in (1, 2, 4):
            incl = incl + jnp.where(sub + sh < SUBLANES, pltpu.roll(incl, SUBLANES - sh, axis=0), 0.0)
        offs = incl - run + carry_sc[...]
        w = jnp.exp2(piece_sc[...].reshape(groups, SUBLANES, blk) + offs[None]).reshape(blk, blk)
        if masked:
            r = lax.broadcasted_iota(jnp.int32, (blk, blk), 0)
            w = jnp.where(_key_of_row(r, blk) < lax.broadcasted_iota(jnp.int32, (blk, blk), 1), w, 0.0)
        w_sc[dst] = w.astype(BF16)
        carry_sc[...] += jnp.broadcast_to(incl[0:1], (SUBLANES, blk))

    def accumulate(kb, slot):
        acc_sc[...] += jnp.dot(vt_ref[0, 0, kb], w_sc[slot], preferred_element_type=F32)

    def start():
        acc_sc[...] = jnp.zeros(acc_sc.shape, F32)
        carry_sc[...] = jnp.zeros(carry_sc.shape, F32)
        z_sc[1] = scores(i)
        weights(1, 0, True)
        z_sc[0] = scores(jnp.maximum(i - 1, 0))

    def put_scores(slot, kb):
        z_sc[slot] = scores(kb)

    def trip(kb, cur):
        return (lambda: put_scores(1 - cur, jnp.maximum(kb - 1, 0)),
                lambda: weights(cur, 1 - cur, False),
                lambda: accumulate(kb + 1, cur))

    def finish(o_ref):
        accumulate(0, i % 2)
        o_ref[0] = acc_sc[...].T.astype(o_ref.dtype)

    return start, trip, finish


def _mixers_kernel(lq1_ref, lk1_ref, lq2_ref, lk2_ref, g_ref, qa_ref, ka_ref, vta_ref, qb_ref, kb_ref, vtb_ref,
                   oa_ref, ob_ref, m_sc, l_sc, acca_sc, s_sc, p_sc, a_sc, accb_sc, carry_sc, z_sc, w_sc, piece_sc,
                   *, blk, lambda_init):
    i = pl.program_id(2)
    a_start, a_trip, a_finish = _diff_parts(i, qa_ref, ka_ref, vta_ref, m_sc, l_sc, acca_sc, s_sc, p_sc, a_sc,
                                            blk=blk)
    b_start, b_trip, b_finish = _sb_parts(i, qb_ref, kb_ref, vtb_ref, accb_sc, carry_sc, z_sc, w_sc, piece_sc,
                                          blk=blk)
    a_start()
    b_start()

    def trip(kb, cur):
        a_scores, a_softmax, a_acc = a_trip(kb, cur)
        b_scores, b_weights, b_acc = b_trip(kb, cur)
        for stage in (a_scores, b_weights, a_acc, b_scores, a_softmax, b_acc):
            stage()

    def loop_body(t, carry):
        for cur in range(2):
            pl.when(t % 2 == cur)(functools.partial(trip, i - 1 - t, cur))
        return carry

    lax.fori_loop(0, i, loop_body, 0)
    a_finish(lq1_ref, lk1_ref, lq2_ref, lk2_ref, g_ref, oa_ref, lambda_init)
    b_finish(ob_ref)


def _mixers(qa_t, ka, va_t, qb_t, kb_p, vb_t, lam_params, norm_g, *, blk, lambda_init):
    b, width, s = qa_t.shape
    hw = HEAD_COLS
    n_heads = width // hw
    nblk = s // blk
    vec_spec = pl.BlockSpec((1, DA_HEAD_DIM), lambda bi, h, i: (0, 0))
    q_spec = pl.BlockSpec((1, hw, blk), lambda bi, h, i: (bi, h, i))
    k_spec = pl.BlockSpec((1, s, hw), lambda bi, h, i: (bi, 0, h), pipeline_mode=pl.Buffered(1))
    v_spec = pl.BlockSpec((1, 1, nblk, hw, blk), lambda bi, h, i: (bi, h, 0, 0, 0), pipeline_mode=pl.Buffered(1))
    o_spec = pl.BlockSpec((1, blk, hw), lambda bi, h, i: (bi, i, h))
    kern = functools.partial(_mixers_kernel, blk=blk, lambda_init=lambda_init)
    out = jax.ShapeDtypeStruct((b, s, width), BF16)
    return pl.pallas_call(
        kern,
        out_shape=(out, out),
        grid=(b, n_heads, nblk),
        in_specs=[vec_spec, vec_spec, vec_spec, vec_spec,
                  pl.BlockSpec((1, hw), lambda bi, h, i: (0, 0)),
                  q_spec, k_spec, v_spec, q_spec, k_spec, v_spec],
        out_specs=(o_spec, o_spec),
        scratch_shapes=[pltpu.VMEM((2, SUBLANES, blk), F32),
                        pltpu.VMEM((2, SUBLANES, blk), F32),
                        pltpu.VMEM((2, hw, blk), F32),
                        pltpu.VMEM((2, 2, blk, blk), F32),
                        pltpu.VMEM((2, 2, blk, blk), BF16),
                        pltpu.VMEM((2, 2, SUBLANES, blk), F32),
                        pltpu.VMEM((hw, blk), F32),
                        pltpu.VMEM((SUBLANES, blk), F32),
                        pltpu.VMEM((2, blk, blk), F32),
                        pltpu.VMEM((2, blk, blk), BF16),
                        pltpu.VMEM((blk, blk), F32)],
        compiler_params=pltpu.CompilerParams(
            dimension_semantics=("parallel", "parallel", "arbitrary"),
            vmem_limit_bytes=_vmem_limit(58 * 1024 * 1024)),
        name="mixers",
    )(*lam_params, norm_g, qa_t, ka, va_t, qb_t, kb_p, vb_t)


def _merge_kernel(ya_ref, yb_ref, g_ref, x_ref, wa_ref, wb_ref, wo_ref, lg_ref, lb_ref, o_ref, *, alpha):
    d = x_ref.shape[1]
    pa = jnp.dot(ya_ref[...], wa_ref[...], preferred_element_type=F32)
    pb = jnp.dot(yb_ref[...], wb_ref[...], preferred_element_type=F32)
    mixed = g_ref[:, :d] * pa + g_ref[:, d:] * pb
    z = jnp.dot(mixed.astype(BF16), wo_ref[...], preferred_element_type=F32)
    o_ref[...] = _layer_norm(alpha * x_ref[...] + z, lg_ref[...], lb_ref[...])


def _merge(ya, yb, gates, x2d, wa, wb, wo, ln_g, ln_b, *, tm, alpha):
    t, d = x2d.shape
    row = lambda width: pl.BlockSpec((tm, width), lambda i: (i, 0))
    return pl.pallas_call(
        functools.partial(_merge_kernel, alpha=alpha),
        out_shape=jax.ShapeDtypeStruct((t, d), F32),
        grid=(t // tm,),
        in_specs=[row(ya.shape[1]), row(yb.shape[1]), row(gates.shape[1]), row(d),
                  _const_spec(wa.shape), _const_spec(wb.shape), _const_spec(wo.shape),
                  _const_spec(ln_g.shape), _const_spec(ln_b.shape)],
        out_specs=row(d),
        compiler_params=pltpu.CompilerParams(
            dimension_semantics=("parallel",),
            vmem_limit_bytes=_vmem_limit(56 * 1024 * 1024)),
        name="merge_ln1",
    )(ya, yb, gates, x2d, wa, wb, wo, ln_g, ln_b)


def _cross_kernel(x_ref, wq_ref, km_ref, vm_ref, wo_ref, lg_ref, lb_ref, o_ref, *, alpha, n_heads):
    x = x_ref[...]
    d = x.shape[1]
    dh = d // n_heads
    q = jnp.dot(x.astype(BF16), wq_ref[...], preferred_element_type=F32).astype(BF16)
    km = km_ref[0]
    vm = vm_ref[0]
    outs = []
    for h in range(n_heads):
        s = lax.dot_general(q[:, h * dh:(h + 1) * dh], km[:, h * dh:(h + 1) * dh],
                            (((1,), (1,)), ((), ())), preferred_element_type=F32) * (dh ** -0.5)
        s = s - jnp.max(s, axis=-1, keepdims=True)
        p = jnp.exp(s)
        p = p / jnp.sum(p, axis=-1, keepdims=True)
        outs.append(jnp.dot(p.astype(BF16), vm[:, h * dh:(h + 1) * dh], preferred_element_type=F32))
    o = jnp.concatenate(outs, axis=-1).astype(BF16)
    c = jnp.dot(o, wo_ref[...], preferred_element_type=F32)
    o_ref[...] = _layer_norm(alpha * x + c, lg_ref[...], lb_ref[...])


def _cross_attention(x2d, wq, kmem, vmem, wo, ln_g, ln_b, *, tm, seq, alpha):
    t, d = x2d.shape
    m = kmem.shape[1]
    per_seq = seq // tm
    row = pl.BlockSpec((tm, d), lambda i: (i, 0))
    mem_spec = pl.BlockSpec((1, m, d), lambda i: (i // per_seq, 0, 0))
    return pl.pallas_call(
        functools.partial(_cross_kernel, alpha=alpha, n_heads=MEM_HEADS),
        out_shape=jax.ShapeDtypeStruct((t, d), F32),
        grid=(t // tm,),
        in_specs=[row, _const_spec(wq.shape), mem_spec, mem_spec, _const_spec(wo.shape),
                  _const_spec(ln_g.shape), _const_spec(ln_b.shape)],
        out_specs=row,
        compiler_params=pltpu.CompilerParams(
            dimension_semantics=("parallel",),
            vmem_limit_bytes=_vmem_limit(56 * 1024 * 1024)),
        name="cross_ln2",
    )(x2d, wq, kmem, vmem, wo, ln_g, ln_b)


def _swiglu_kernel(x_ref, wg_ref, wu_ref, wd_ref, lg_ref, lb_ref, o_ref, xb_sc, acc_sc, *, alpha):
    j = pl.program_id(1)

    @pl.when(j == 0)
    def _():
        xb_sc[...] = x_ref[...].astype(BF16)
        acc_sc[...] = jnp.zeros(acc_sc.shape, F32)

    xb = xb_sc[...]
    g = jnp.dot(xb, wg_ref[...], preferred_element_type=F32)
    u = jnp.dot(xb, wu_ref[...], preferred_element_type=F32)
    hmid = (g * jax.nn.sigmoid(g) * u).astype(BF16)
    acc_sc[...] += jnp.dot(hmid, wd_ref[...], preferred_element_type=F32)

    @pl.when(j == pl.num_programs(1) - 1)
    def _():
        o_ref[...] = _layer_norm(alpha * x_ref[...] + acc_sc[...], lg_ref[...], lb_ref[...])


def _swiglu(x2d, wg, wu, wd, ln_g, ln_b, *, tm, tf, alpha):
    t, d = x2d.shape
    dff = wg.shape[1]
    row = pl.BlockSpec((tm, d), lambda i, j: (i, 0))
    vec = pl.BlockSpec((1, d), lambda i, j: (0, 0))
    return pl.pallas_call(
        functools.partial(_swiglu_kernel, alpha=alpha),
        out_shape=jax.ShapeDtypeStruct((t, d), F32),
        grid=(t // tm, dff // tf),
        in_specs=[row,
                  pl.BlockSpec((d, tf), lambda i, j: (0, j)),
                  pl.BlockSpec((d, tf), lambda i, j: (0, j)),
                  pl.BlockSpec((tf, d), lambda i, j: (j, 0)),
                  vec, vec],
        out_specs=row,
        scratch_shapes=[pltpu.VMEM((tm, d), BF16), pltpu.VMEM((tm, d), F32)],
        compiler_params=pltpu.CompilerParams(
            dimension_semantics=("parallel", "arbitrary"),
            vmem_limit_bytes=_vmem_limit(56 * 1024 * 1024)),
        name="swiglu_ln3",
    )(x2d, wg, wu, wd, ln_g, ln_b)


def _rope_tables(seq):
    inv = ROPE_THETA ** (-jnp.arange(0, DA_HEAD_DIM, 2, dtype=F32) / DA_HEAD_DIM)
    ang = jnp.arange(seq, dtype=F32)[:, None] * inv[None, :]
    cos, sin = jnp.cos(ang), jnp.sin(ang)
    return jnp.concatenate([cos, cos], axis=-1), jnp.concatenate([-sin, sin], axis=-1)


def kernel(x, mem, w_in, b_gate, lam_q1, lam_k1, lam_q2, lam_k2, da_norm_g, w_proj_a, w_proj_b, w_out,
           ln1_g, ln1_b, w_mq, w_mk, w_mv, w_mo, ln2_g, ln2_b, w_gate, w_up, w_down, ln3_g, ln3_b):
    bsz, seq, d = x.shape
    depth = w_in.shape[0]
    t = bsz * seq
    da_width = DA_HEADS * 2 * DA_HEAD_DIM
    sb_width = SB_HEADS * SB_HEAD_DIM
    alpha = (2.0 * depth) ** 0.25
    blk = min(ATTN_BLK, seq)
    rope = _rope_tables(seq)
    tm_proj = min(1024, seq)
    tm_tok = min(256, seq)
    tm_ff = min(512, seq)
    row_vec = lambda p: p.reshape(1, -1).astype(F32)

    x2d = x.reshape(t, d)
    for l in range(depth):
        lambda_init = 0.8 - 0.6 * math.exp(-0.3 * l)
        w_l = w_in[l].astype(BF16)
        proj_any = functools.partial(_projection, bsz=bsz, seq=seq, blk=blk, tm=tm_proj, tn=1024)
        bounds = [0, da_width, 2 * da_width, 3 * da_width, 3 * da_width + sb_width,
                  3 * da_width + 2 * sb_width, 3 * da_width + 3 * sb_width, w_l.shape[1]]
        w_qa, w_ka, w_va, w_qb, w_kb, w_vb, w_g = [w_l[:, lo:hi] for lo, hi in zip(bounds[:-1], bounds[1:])]
        qa_t, xb = proj_any(x2d, w_qa, name="proj_qa", rope=rope, scale=DA_HEAD_DIM ** -0.5 * LOG2E,
                            layout="cols", emit_x=True)
        proj = functools.partial(proj_any, xb)
        ka = proj(w_ka, name="proj_ka", rope=rope)
        va_t = proj(w_va, name="proj_va", layout="blocked")
        qb_t = proj(w_qb, name="proj_qb", scale=SB_HEAD_DIM ** -0.5 * LOG2E, layout="cols")
        kb_p = proj(w_kb, name="proj_kb", perm=True)
        vb_t = proj(w_vb, name="proj_vb", perm=True, layout="blocked")
        gates = proj(w_g, name="proj_gate", bias=row_vec(b_gate[l]))

        ya, yb = _mixers(qa_t, ka.reshape(bsz, seq, da_width), va_t,
                         qb_t, kb_p.reshape(bsz, seq, sb_width), vb_t,
                         [row_vec(p[l]) for p in (lam_q1, lam_k1, lam_q2, lam_k2)],
                         row_vec(da_norm_g[l]), blk=blk, lambda_init=lambda_init)

        x2d = _merge(ya.reshape(t, da_width), yb.reshape(t, sb_width), gates, x2d,
                     w_proj_a[l].astype(BF16), w_proj_b[l].astype(BF16), w_out[l].astype(BF16),
                     row_vec(ln1_g[l]), row_vec(ln1_b[l]), tm=tm_tok, alpha=alpha)
        n_mem = mem.shape[1]
        mem2d = mem.reshape(bsz * n_mem, d).astype(BF16)
        w_kv = jnp.concatenate([w_mk[l], w_mv[l]], axis=1).astype(BF16)
        kv = _projection(mem2d, w_kv, name="proj_mem", bsz=bsz, seq=n_mem, blk=blk,
                         tm=min(256, n_mem), tn=1024)
        kmem = kv[:, :d].reshape(bsz, n_mem, d)
        vmem = kv[:, d:].reshape(bsz, n_mem, d)
        x2d = _cross_attention(x2d, w_mq[l].astype(BF16), kmem, vmem, w_mo[l].astype(BF16),
                               row_vec(ln2_g[l]), row_vec(ln2_b[l]), tm=tm_tok, seq=seq, alpha=alpha)
        x2d = _swiglu(x2d, w_gate[l].astype(BF16), w_up[l].astype(BF16), w_down[l].astype(BF16),
                      row_vec(ln3_g[l]), row_vec(ln3_b[l]), tm=tm_ff, tf=512, alpha=alpha)
    return x2d.reshape(bsz, seq, d)
```

```python
import functools
import math

import jax
import jax.numpy as jnp
from jax import lax
from jax.experimental import pallas as pl
from jax.experimental.pallas import tpu as pltpu

F32 = jnp.float32
BF16 = jnp.bfloat16

CHUNK = 64
ROPE_THETA = 10000.0
LN_EPS = 1e-5
DA_HEADS = 4
DA_HEAD_DIM = 128
SB_HEADS = 4
SB_HEAD_DIM = 256
MEM_HEADS = 4
LOG2E = math.log2(math.e)

LANES = 128
SUBLANES = 8
PACK = 2
VMEM_BYTES = 64 * 1024 * 1024

ATTN_BLK = 512
HEAD_COLS = 256


def _vmem_limit(nbytes):
    return int(min(nbytes, VMEM_BYTES - 6 * 1024 * 1024))


def _layer_norm(r, g, b):
    mu = jnp.mean(r, axis=-1, keepdims=True)
    d = r - mu
    var = jnp.mean(d * d, axis=-1, keepdims=True)
    return d * lax.rsqrt(var + LN_EPS) * g + b


def _const_spec(shape):
    nd = len(shape)
    return pl.BlockSpec(shape, lambda *_: (0,) * nd, pipeline_mode=pl.Buffered(1))


def _key_of_row(r, blk):
    return (r % SUBLANES) * (blk // SUBLANES) + r // SUBLANES


def _proj_kernel(*refs, rope, scale, perm, layout, gate, emit_x, blk):
    it = iter(refs)
    x_ref, w_ref = next(it), next(it)
    cos_ref, sin_ref = (next(it), next(it)) if rope else (None, None)
    b_ref = next(it) if gate else None
    o_ref = next(it)
    tm = x_ref.shape[0]
    x = x_ref[...].astype(BF16)
    if emit_x:
        next(it)[...] = x
    if perm:
        r = lax.broadcasted_iota(jnp.int32, (blk, blk), 0)
        c = lax.broadcasted_iota(jnp.int32, (blk, blk), 1)
        pmat = (c == _key_of_row(r, blk)).astype(BF16)
        x = jnp.concatenate(
            [jnp.dot(pmat, x[s * blk:(s + 1) * blk], preferred_element_type=F32).astype(BF16)
             for s in range(tm // blk)], axis=0)
    acc = jnp.dot(x, w_ref[...], preferred_element_type=F32)
    n = acc.shape[1]
    if rope:
        cos = cos_ref[...]
        sin = sin_ref[...]
        heads = []
        for h in range(n // DA_HEAD_DIM):
            t = acc[:, h * DA_HEAD_DIM:(h + 1) * DA_HEAD_DIM]
            heads.append(t * cos + pltpu.roll(t, DA_HEAD_DIM // 2, axis=1) * sin)
        acc = jnp.concatenate(heads, axis=1)
    if scale is not None:
        acc = acc * scale
    if gate:
        acc = jax.nn.sigmoid(acc + b_ref[...])
    if layout == "rows":
        o_ref[...] = acc.astype(o_ref.dtype)
    elif layout == "cols":
        o_ref[0] = acc.T.astype(o_ref.dtype)
    else:
        for h in range(n // HEAD_COLS):
            for s in range(tm // blk):
                tile = acc[s * blk:(s + 1) * blk, h * HEAD_COLS:(h + 1) * HEAD_COLS]
                o_ref[0, h, s] = tile.T.astype(o_ref.dtype)


def _projection(x2d, w, *, name, bsz, seq, blk, tm, tn, out_dtype=BF16, rope=None, scale=None, perm=False,
                layout="rows", bias=None, emit_x=False):
    t, d = x2d.shape
    n = w.shape[1]
    per_seq = seq // tm
    grid = (t // tm, n // tn)
    in_specs = [pl.BlockSpec((tm, d), lambda i, j: (i, 0)),
                pl.BlockSpec((d, tn), lambda i, j: (0, j))]
    args = [x2d, w]
    if rope is not None:
        tab_spec = pl.BlockSpec((tm, DA_HEAD_DIM), lambda i, j: (i % per_seq, 0))
        in_specs += [tab_spec, tab_spec]
        args += list(rope)
    if bias is not None:
        in_specs.append(pl.BlockSpec((1, tn), lambda i, j: (0, j)))
        args.append(bias)
    if layout == "rows":
        out_shape = jax.ShapeDtypeStruct((t, n), out_dtype)
        out_spec = pl.BlockSpec((tm, tn), lambda i, j: (i, j))
    elif layout == "cols":
        out_shape = jax.ShapeDtypeStruct((bsz, n, seq), out_dtype)
        out_spec = pl.BlockSpec((1, tn, tm), lambda i, j: (i // per_seq, j, i % per_seq))
    else:
        out_shape = jax.ShapeDtypeStruct((bsz, n // HEAD_COLS, seq // blk, HEAD_COLS, blk), out_dtype)
        out_spec = pl.BlockSpec((1, tn // HEAD_COLS, tm // blk, HEAD_COLS, blk),
                                lambda i, j: (i // per_seq, j, i % per_seq, 0, 0))
    if emit_x:
        assert n == tn
        out_shape = (out_shape, jax.ShapeDtypeStruct((t, d), BF16))
        out_spec = (out_spec, pl.BlockSpec((tm, d), lambda i, j: (i, 0)))
    kern = functools.partial(_proj_kernel, rope=rope is not None, scale=scale, perm=perm, layout=layout,
                             gate=bias is not None, emit_x=emit_x, blk=blk)
    return pl.pallas_call(
        kern,
        out_shape=out_shape,
        grid=grid,
        in_specs=in_specs,
        out_specs=out_spec,
        compiler_params=pltpu.CompilerParams(
            dimension_semantics=("parallel", "arbitrary"),
            vmem_limit_bytes=_vmem_limit(56 * 1024 * 1024)),
        name=name,
    )(*args)


def _sublane_allmax(v):
    for sh in (4, 2, 1):
        v = jnp.maximum(v, pltpu.roll(v, sh, axis=0))
    return v


def _diff_parts(i, qt_ref, k_ref, vt_ref, m_sc, l_sc, acc_sc, s_sc, p_sc, a_sc, *, blk):
    dh = DA_HEAD_DIM
    groups = blk // SUBLANES

    def scores(j):
        start = pl.multiple_of(j * blk, blk)
        k = k_ref[0, pl.ds(start, blk), :]
        qt = qt_ref[0]
        return [jnp.dot(k[:, c * dh:(c + 1) * dh], qt[c * dh:(c + 1) * dh, :], preferred_element_type=F32)
                for c in range(2)]

    def put(ref, slot, pair):
        for c in range(2):
            ref[slot, c] = pair[c]

    def softmax(src, dst, masked):
        if masked:
            key = lax.broadcasted_iota(jnp.int32, (blk, blk), 0)
            qry = lax.broadcasted_iota(jnp.int32, (blk, blk), 1)
            allowed = (key // CHUNK) <= (qry // CHUNK)
        for c in range(2):
            s = s_sc[src, c]
            if masked:
                s = jnp.where(allowed, s, -jnp.inf)
            s3 = s.reshape(groups, SUBLANES, blk)
            m_prev = m_sc[c]
            m_new = jnp.maximum(m_prev, _sublane_allmax(jnp.max(s3, axis=0)))
            alpha = jnp.exp2(m_prev - m_new)
            m_sc[c] = m_new
            a_sc[dst, c] = alpha
            m_slab = jnp.concatenate([m_new] * PACK, axis=0)
            l_part = jnp.zeros((SUBLANES, blk), F32)
            for g in range(0, groups, PACK):
                rows = pl.ds(g * SUBLANES, PACK * SUBLANES)
                s_slab = s_sc[src, c, rows, :]
                if masked:
                    s_slab = jnp.where(allowed[g * SUBLANES:(g + PACK) * SUBLANES], s_slab, -jnp.inf)
                p = jnp.exp2(s_slab - m_slab)
                for h in range(PACK):
                    l_part = l_part + p[h * SUBLANES:(h + 1) * SUBLANES]
                p_sc[dst, c, rows, :] = p.astype(BF16)
            l_sc[c] = alpha * l_sc[c] + l_part

    def accumulate(j, slot):
        vt = vt_ref[0, 0, j]
        for c in range(2):
            pv = jnp.dot(vt, p_sc[slot, c], preferred_element_type=F32)
            acc_sc[c] = acc_sc[c] * a_sc[slot, c][0:1] + pv

    def init():
        m_sc[...] = jnp.full(m_sc.shape, -jnp.inf, F32)
        l_sc[...] = jnp.zeros(l_sc.shape, F32)
        acc_sc[...] = jnp.zeros(acc_sc.shape, F32)
        put(s_sc, 1, scores(i))

    start = (init,
             lambda: softmax(1, 0, True),
             lambda: put(s_sc, 0, scores(jnp.maximum(i - 1, 0))))

    def trip(kb, cur):
        return (lambda: put(s_sc, 1 - cur, scores(jnp.maximum(kb - 1, 0))),
                lambda: softmax(cur, 1 - cur, False),
                lambda: accumulate(kb + 1, cur))

    def normalize(lq1_ref, lk1_ref, lq2_ref, lk2_ref, g_ref, o_ref, lambda_init):
        lam = (jnp.exp(jnp.sum(lq1_ref[...] * lk1_ref[...], axis=-1, keepdims=True))
               - jnp.exp(jnp.sum(lq2_ref[...] * lk2_ref[...], axis=-1, keepdims=True))
               + lambda_init)
        l0 = jnp.sum(l_sc[0], axis=0, keepdims=True)
        l1 = jnp.sum(l_sc[1], axis=0, keepdims=True)
        o = (acc_sc[0] / l0 - lam * (acc_sc[1] / l1)).T
        o = o * lax.rsqrt(jnp.mean(o * o, axis=-1, keepdims=True) + LN_EPS)
        o = o * g_ref[...] * (1.0 - lambda_init)
        o_ref[0] = o.astype(o_ref.dtype)

    finish = (lambda *_: accumulate(0, i % 2), normalize)
    return start, trip, finish


def _sb_parts(i, qt_ref, k_ref, vt_ref, acc_sc, carry_sc, z_sc, w_sc, piece_sc, *, blk):
    groups = blk // SUBLANES

    def scores(kb):
        start = pl.multiple_of(kb * blk, blk)
        return jnp.dot(k_ref[0, pl.ds(start, blk), :], qt_ref[0], preferred_element_type=F32)

    def weights(src, dst, masked):
        sub = lax.broadcasted_iota(jnp.int32, (SUBLANES, blk), 0)
        qpos = lax.broadcasted_iota(jnp.int32, (SUBLANES, blk), 1)
        run = jnp.zeros((SUBLANES, blk), F32)
        for a in reversed(range(groups)):
            rows = pl.ds(a * SUBLANES, SUBLANES)
            z = z_sc[src, rows, :]
            neg_abs = pltpu.bitcast(pltpu.bitcast(z, jnp.uint32) | jnp.uint32(0x80000000), F32)
            ls_pos = jnp.minimum(z, 0.0) - jnp.log(1.0 + jnp.exp2(neg_abs)) * LOG2E
            piece = ls_pos + run
            if masked:
                run = run + jnp.where(sub * groups + a < qpos, ls_pos - z, 0.0)
            else:
                run = piece - z
            piece_sc[rows, :] = piece
        incl = run
        for sh in (1, 2, 4):
            incl = incl + jnp.where(sub + sh < SUBLANES, pltpu.roll(incl, SUBLANES - sh, axis=0), 0.0)
        offs = incl - run + carry_sc[...]
        carry_sc[...] += jnp.broadcast_to(incl[0:1], (SUBLANES, blk))
        offs_slab = jnp.concatenate([offs] * PACK, axis=0)
        for g in range(0, groups, PACK):
            rows = pl.ds(g * SUBLANES, PACK * SUBLANES)
            w = jnp.exp2(piece_sc[rows, :] + offs_slab)
            if masked:
                r = lax.broadcasted_iota(jnp.int32, (PACK * SUBLANES, blk), 0) + g * SUBLANES
                qp = lax.broadcasted_iota(jnp.int32, (PACK * SUBLANES, blk), 1)
                w = jnp.where(_key_of_row(r, blk) < qp, w, 0.0)
            w_sc[dst, rows, :] = w.astype(BF16)

    def accumulate(kb, slot):
        acc_sc[...] += jnp.dot(vt_ref[0, 0, kb], w_sc[slot], preferred_element_type=F32)

    def put_scores(slot, kb):
        z_sc[slot] = scores(kb)

    def init():
        acc_sc[...] = jnp.zeros(acc_sc.shape, F32)
        carry_sc[...] = jnp.zeros(carry_sc.shape, F32)
        put_scores(1, i)

    start = (init,
             lambda: weights(1, 0, True),
             lambda: put_scores(0, jnp.maximum(i - 1, 0)))

    def trip(kb, cur):
        return (lambda: put_scores(1 - cur, jnp.maximum(kb - 1, 0)),
                lambda: weights(cur, 1 - cur, False),
                lambda: accumulate(kb + 1, cur))

    def write_out(o_ref):
        o_ref[0] = acc_sc[...].T.astype(o_ref.dtype)

    finish = (lambda *_: accumulate(0, i % 2), write_out)
    return start, trip, finish


def _mixers_kernel(lq1_ref, lk1_ref, lq2_ref, lk2_ref, g_ref, qa_ref, ka_ref, vta_ref, qb_ref, kb_ref, vtb_ref,
                   oa_ref, ob_ref, m_sc, l_sc, acca_sc, s_sc, p_sc, a_sc, accb_sc, carry_sc, z_sc, w_sc, piece_sc,
                   *, blk, lambda_init):
    i = pl.program_id(2)
    a_start, a_trip, a_finish = _diff_parts(i, qa_ref, ka_ref, vta_ref, m_sc, l_sc, acca_sc, s_sc, p_sc, a_sc,
                                            blk=blk)
    b_start, b_trip, b_finish = _sb_parts(i, qb_ref, kb_ref, vtb_ref, accb_sc, carry_sc, z_sc, w_sc, piece_sc,
                                          blk=blk)
    for a_stage, b_stage in zip(a_start, b_start):
        a_stage()
        b_stage()

    def trip(kb, cur):
        a_scores, a_softmax, a_acc = a_trip(kb, cur)
        b_scores, b_weights, b_acc = b_trip(kb, cur)
        for stage in (a_scores, b_weights, a_acc, b_scores, a_softmax, b_acc):
            stage()

    def loop_body(t, carry):
        for cur in range(2):
            pl.when(t % 2 == cur)(functools.partial(trip, i - 1 - t, cur))
        return carry

    lax.fori_loop(0, i, loop_body, 0)
    for a_stage, b_stage in zip(a_finish, b_finish):
        a_stage(lq1_ref, lk1_ref, lq2_ref, lk2_ref, g_ref, oa_ref, lambda_init)
        b_stage(ob_ref)


def _mixers(qa_t, ka, va_t, qb_t, kb_p, vb_t, lam_params, norm_g, *, blk, lambda_init):
    b, width, s = qa_t.shape
    hw = HEAD_COLS
    n_heads = width // hw
    nblk = s // blk
    vec_spec = pl.BlockSpec((1, DA_HEAD_DIM), lambda bi, h, i: (0, 0))
    q_spec = pl.BlockSpec((1, hw, blk), lambda bi, h, i: (bi, h, i))
    k_spec = pl.BlockSpec((1, s, hw), lambda bi, h, i: (bi, 0, h), pipeline_mode=pl.Buffered(1))
    v_spec = pl.BlockSpec((1, 1, nblk, hw, blk), lambda bi, h, i: (bi, h, 0, 0, 0), pipeline_mode=pl.Buffered(1))
    o_spec = pl.BlockSpec((1, blk, hw), lambda bi, h, i: (bi, i, h))
    kern = functools.partial(_mixers_kernel, blk=blk, lambda_init=lambda_init)
    out = jax.ShapeDtypeStruct((b, s, width), BF16)
    return pl.pallas_call(
        kern,
        out_shape=(out, out),
        grid=(b, n_heads, nblk),
        in_specs=[vec_spec, vec_spec, vec_spec, vec_spec,
                  pl.BlockSpec((1, hw), lambda bi, h, i: (0, 0)),
                  q_spec, k_spec, v_spec, q_spec, k_spec, v_spec],
        out_specs=(o_spec, o_spec),
        scratch_shapes=[pltpu.VMEM((2, SUBLANES, blk), F32),
                        pltpu.VMEM((2, SUBLANES, blk), F32),
                        pltpu.VMEM((2, hw, blk), F32),
                        pltpu.VMEM((2, 2, blk, blk), F32),
                        pltpu.VMEM((2, 2, blk, blk), BF16),
                        pltpu.VMEM((2, 2, SUBLANES, blk), F32),
                        pltpu.VMEM((hw, blk), F32),
                        pltpu.VMEM((SUBLANES, blk), F32),
                        pltpu.VMEM((2, blk, blk), F32),
                        pltpu.VMEM((2, blk, blk), BF16),
                        pltpu.VMEM((blk, blk), F32)],
        compiler_params=pltpu.CompilerParams(
            dimension_semantics=("parallel", "parallel", "arbitrary"),
            vmem_limit_bytes=_vmem_limit(58 * 1024 * 1024)),
        name="mixers",
    )(*lam_params, norm_g, qa_t, ka, va_t, qb_t, kb_p, vb_t)


def _merge_kernel(ya_ref, yb_ref, g_ref, x_ref, wa_ref, wb_ref, wo_ref, lg_ref, lb_ref, o_ref, *, alpha):
    d = x_ref.shape[1]
    pa = jnp.dot(ya_ref[...], wa_ref[...], preferred_element_type=F32)
    pb = jnp.dot(yb_ref[...], wb_ref[...], preferred_element_type=F32)
    mixed = g_ref[:, :d] * pa + g_ref[:, d:] * pb
    z = jnp.dot(mixed.astype(BF16), wo_ref[...], preferred_element_type=F32)
    o_ref[...] = _layer_norm(alpha * x_ref[...] + z, lg_ref[...], lb_ref[...])


def _merge(ya, yb, gates, x2d, wa, wb, wo, ln_g, ln_b, *, tm, alpha):
    t, d = x2d.shape
    row = lambda width: pl.BlockSpec((tm, width), lambda i: (i, 0))
    return pl.pallas_call(
        functools.partial(_merge_kernel, alpha=alpha),
        out_shape=jax.ShapeDtypeStruct((t, d), F32),
        grid=(t // tm,),
        in_specs=[row(ya.shape[1]), row(yb.shape[1]), row(gates.shape[1]), row(d),
                  _const_spec(wa.shape), _const_spec(wb.shape), _const_spec(wo.shape),
                  _const_spec(ln_g.shape), _const_spec(ln_b.shape)],
        out_specs=row(d),
        compiler_params=pltpu.CompilerParams(
            dimension_semantics=("parallel",),
            vmem_limit_bytes=_vmem_limit(56 * 1024 * 1024)),
        name="merge_ln1",
    )(ya, yb, gates, x2d, wa, wb, wo, ln_g, ln_b)


def _cross_kernel(x_ref, wq_ref, km_ref, vm_ref, wo_ref, lg_ref, lb_ref, o_ref, *, alpha, n_heads):
    x = x_ref[...]
    d = x.shape[1]
    dh = d // n_heads
    q = jnp.dot(x.astype(BF16), wq_ref[...], preferred_element_type=F32).astype(BF16)
    km = km_ref[0]
    vm = vm_ref[0]
    outs = []
    for h in range(n_heads):
        s = lax.dot_general(q[:, h * dh:(h + 1) * dh], km[:, h * dh:(h + 1) * dh],
                            (((1,), (1,)), ((), ())), preferred_element_type=F32) * (dh ** -0.5)
        s = s - jnp.max(s, axis=-1, keepdims=True)
        p = jnp.exp(s)
        p = p / jnp.sum(p, axis=-1, keepdims=True)
        outs.append(jnp.dot(p.astype(BF16), vm[:, h * dh:(h + 1) * dh], preferred_element_type=F32))
    o = jnp.concatenate(outs, axis=-1).astype(BF16)
    c = jnp.dot(o, wo_ref[...], preferred_element_type=F32)
    o_ref[...] = _layer_norm(alpha * x + c, lg_ref[...], lb_ref[...])


def _cross_attention(x2d, wq, kmem, vmem, wo, ln_g, ln_b, *, tm, seq, alpha):
    t, d = x2d.shape
    m = kmem.shape[1]
    per_seq = seq // tm
    row = pl.BlockSpec((tm, d), lambda i: (i, 0))
    mem_spec = pl.BlockSpec((1, m, d), lambda i: (i // per_seq, 0, 0))
    return pl.pallas_call(
        functools.partial(_cross_kernel, alpha=alpha, n_heads=MEM_HEADS),
        out_shape=jax.ShapeDtypeStruct((t, d), F32),
        grid=(t // tm,),
        in_specs=[row, _const_spec(wq.shape), mem_spec, mem_spec, _const_spec(wo.shape),
                  _const_spec(ln_g.shape), _const_spec(ln_b.shape)],
        out_specs=row,
        compiler_params=pltpu.CompilerParams(
            dimension_semantics=("parallel",),
            vmem_limit_bytes=_vmem_limit(56 * 1024 * 1024)),
        name="cross_ln2",
    )(x2d, wq, kmem, vmem, wo, ln_g, ln_b)


def _swiglu_kernel(x_ref, wg_ref, wu_ref, wd_ref, lg_ref, lb_ref, o_ref, xb_sc, acc_sc, *, alpha):
    j = pl.program_id(1)

    @pl.when(j == 0)
    def _():
        xb_sc[...] = x_ref[...].astype(BF16)
        acc_sc[...] = jnp.zeros(acc_sc.shape, F32)

    xb = xb_sc[...]
    g = jnp.dot(xb, wg_ref[...], preferred_element_type=F32)
    u = jnp.dot(xb, wu_ref[...], preferred_element_type=F32)
    hmid = (g * jax.nn.sigmoid(g) * u).astype(BF16)
    acc_sc[...] += jnp.dot(hmid, wd_ref[...], preferred_element_type=F32)

    @pl.when(j == pl.num_programs(1) - 1)
    def _():
        o_ref[...] = _layer_norm(alpha * x_ref[...] + acc_sc[...], lg_ref[...], lb_ref[...])


def _swiglu(x2d, wg, wu, wd, ln_g, ln_b, *, tm, tf, alpha):
    t, d = x2d.shape
    dff = wg.shape[1]
    row = pl.BlockSpec((tm, d), lambda i, j: (i, 0))
    vec = pl.BlockSpec((1, d), lambda i, j: (0, 0))
    return pl.pallas_call(
        functools.partial(_swiglu_kernel, alpha=alpha),
        out_shape=jax.ShapeDtypeStruct((t, d), F32),
        grid=(t // tm, dff // tf),
        in_specs=[row,
                  pl.BlockSpec((d, tf), lambda i, j: (0, j)),
                  pl.BlockSpec((d, tf), lambda i, j: (0, j)),
                  pl.BlockSpec((tf, d), lambda i, j: (j, 0)),
                  vec, vec],
        out_specs=row,
        scratch_shapes=[pltpu.VMEM((tm, d), BF16), pltpu.VMEM((tm, d), F32)],
        compiler_params=pltpu.CompilerParams(
            dimension_semantics=("parallel", "arbitrary"),
            vmem_limit_bytes=_vmem_limit(56 * 1024 * 1024)),
        name="swiglu_ln3",
    )(x2d, wg, wu, wd, ln_g, ln_b)


def _rope_tables(seq):
    inv = ROPE_THETA ** (-jnp.arange(0, DA_HEAD_DIM, 2, dtype=F32) / DA_HEAD_DIM)
    ang = jnp.arange(seq, dtype=F32)[:, None] * inv[None, :]
    cos, sin = jnp.cos(ang), jnp.sin(ang)
    return jnp.concatenate([cos, cos], axis=-1), jnp.concatenate([-sin, sin], axis=-1)


def kernel(x, mem, w_in, b_gate, lam_q1, lam_k1, lam_q2, lam_k2, da_norm_g, w_proj_a, w_proj_b, w_out,
           ln1_g, ln1_b, w_mq, w_mk, w_mv, w_mo, ln2_g, ln2_b, w_gate, w_up, w_down, ln3_g, ln3_b):
    bsz, seq, d = x.shape
    depth = w_in.shape[0]
    t = bsz * seq
    da_width = DA_HEADS * 2 * DA_HEAD_DIM
    sb_width = SB_HEADS * SB_HEAD_DIM
    alpha = (2.0 * depth) ** 0.25
    blk = min(ATTN_BLK, seq)
    rope = _rope_tables(seq)
    tm_proj = min(1024, seq)
    tm_tok = min(256, seq)
    tm_ff = min(512, seq)
    row_vec = lambda p: p.reshape(1, -1).astype(F32)

    x2d = x.reshape(t, d)
    for l in range(depth):
        lambda_init = 0.8 - 0.6 * math.exp(-0.3 * l)
        w_l = w_in[l].astype(BF16)
        proj_any = functools.partial(_projection, bsz=bsz, seq=seq, blk=blk, tm=tm_proj, tn=1024)
        bounds = [0, da_width, 2 * da_width, 3 * da_width, 3 * da_width + sb_width,
                  3 * da_width + 2 * sb_width, 3 * da_width + 3 * sb_width, w_l.shape[1]]
        w_qa, w_ka, w_va, w_qb, w_kb, w_vb, w_g = [w_l[:, lo:hi] for lo, hi in zip(bounds[:-1], bounds[1:])]
        qa_t, xb = proj_any(x2d, w_qa, name="proj_qa", rope=rope, scale=DA_HEAD_DIM ** -0.5 * LOG2E,
                            layout="cols", emit_x=True)
        proj = functools.partial(proj_any, xb)
        ka = proj(w_ka, name="proj_ka", rope=rope)
        va_t = proj(w_va, name="proj_va", layout="blocked")
        qb_t = proj(w_qb, name="proj_qb", scale=SB_HEAD_DIM ** -0.5 * LOG2E, layout="cols")
        kb_p = proj(w_kb, name="proj_kb", perm=True)
        vb_t = proj(w_vb, name="proj_vb", perm=True, layout="blocked")
        gates = proj(w_g, name="proj_gate", bias=row_vec(b_gate[l]))

        ya, yb = _mixers(qa_t, ka.reshape(bsz, seq, da_width), va_t,
                         qb_t, kb_p.reshape(bsz, seq, sb_width), vb_t,
                         [row_vec(p[l]) for p in (lam_q1, lam_k1, lam_q2, lam_k2)],
                         row_vec(da_norm_g[l]), blk=blk, lambda_init=lambda_init)

        x2d = _merge(ya.reshape(t, da_width), yb.reshape(t, sb_width), gates, x2d,
                     w_proj_a[l].astype(BF16), w_proj_b[l].astype(BF16), w_out[l].astype(BF16),
                     row_vec(ln1_g[l]), row_vec(ln1_b[l]), tm=tm_tok, alpha=alpha)
        n_mem = mem.shape[1]
        mem2d = mem.reshape(bsz * n_mem, d).astype(BF16)
        w_kv = jnp.concatenate([w_mk[l], w_mv[l]], axis=1).astype(BF16)
        kv = _projection(mem2d, w_kv, name="proj_mem", bsz=bsz, seq=n_mem, blk=blk,
                         tm=min(256, n_mem), tn=1024)
        kmem = kv[:, :d].reshape(bsz, n_mem, d)
        vmem = kv[:, d:].reshape(bsz, n_mem, d)
        x2d = _cross_attention(x2d, w_mq[l].astype(BF16), kmem, vmem, w_mo[l].astype(BF16),
                               row_vec(ln2_g[l]), row_vec(ln2_b[l]), tm=tm_tok, seq=seq, alpha=alpha)
        x2d = _swiglu(x2d, w_gate[l].astype(BF16), w_up[l].astype(BF16), w_down[l].astype(BF16),
                      row_vec(ln3_g[l]), row_vec(ln3_b[l]), tm=tm_ff, tf=512, alpha=alpha)
    return x2d.reshape(bsz, seq, d)
```

```python
import functools
import math

import jax
import jax.numpy as jnp
from jax import lax
from jax.experimental import pallas as pl
from jax.experimental.pallas import tpu as pltpu

F32 = jnp.float32
BF16 = jnp.bfloat16

CHUNK = 64
ROPE_THETA = 10000.0
LN_EPS = 1e-5
DA_HEADS = 4
DA_HEAD_DIM = 128
SB_HEADS = 4
SB_HEAD_DIM = 256
MEM_HEADS = 4
LOG2E = math.log2(math.e)

LANES = 128
SUBLANES = 8
PACK = 2
VMEM_BYTES = 64 * 1024 * 1024

ATTN_BLK = 512
HEAD_COLS = 256


def _vmem_limit(nbytes):
    return int(min(nbytes, VMEM_BYTES - 6 * 1024 * 1024))


def _layer_norm(r, g, b):
    mu = jnp.mean(r, axis=-1, keepdims=True)
    d = r - mu
    var = jnp.mean(d * d, axis=-1, keepdims=True)
    return d * lax.rsqrt(var + LN_EPS) * g + b


def _const_spec(shape):
    nd = len(shape)
    return pl.BlockSpec(shape, lambda *_: (0,) * nd, pipeline_mode=pl.Buffered(1))


def _key_of_row(r, blk):
    return (r % SUBLANES) * (blk // SUBLANES) + r // SUBLANES


def _proj_kernel(*refs, rope, scale, perm, layout, gate, emit_x, blk):
    it = iter(refs)
    x_ref, w_ref = next(it), next(it)
    cos_ref, sin_ref = (next(it), next(it)) if rope else (None, None)
    b_ref = next(it) if gate else None
    o_ref = next(it)
    tm = x_ref.shape[0]
    x = x_ref[...].astype(BF16)
    if emit_x:
        next(it)[...] = x
    if perm:
        r = lax.broadcasted_iota(jnp.int32, (blk, blk), 0)
        c = lax.broadcasted_iota(jnp.int32, (blk, blk), 1)
        pmat = (c == _key_of_row(r, blk)).astype(BF16)
        x = jnp.concatenate(
            [jnp.dot(pmat, x[s * blk:(s + 1) * blk], preferred_element_type=F32).astype(BF16)
             for s in range(tm // blk)], axis=0)
    acc = jnp.dot(x, w_ref[...], preferred_element_type=F32)
    n = acc.shape[1]
    if rope:
        cos = cos_ref[...]
        sin = sin_ref[...]
        heads = []
        for h in range(n // DA_HEAD_DIM):
            t = acc[:, h * DA_HEAD_DIM:(h + 1) * DA_HEAD_DIM]
            heads.append(t * cos + pltpu.roll(t, DA_HEAD_DIM // 2, axis=1) * sin)
        acc = jnp.concatenate(heads, axis=1)
    if scale is not None:
        acc = acc * scale
    if gate:
        acc = jax.nn.sigmoid(acc + b_ref[...])
    if layout == "rows":
        o_ref[...] = acc.astype(o_ref.dtype)
    elif layout == "cols":
        o_ref[0] = acc.T.astype(o_ref.dtype)
    else:
        for h in range(n // HEAD_COLS):
            for s in range(tm // blk):
                tile = acc[s * blk:(s + 1) * blk, h * HEAD_COLS:(h + 1) * HEAD_COLS]
                o_ref[0, h, s] = tile.T.astype(o_ref.dtype)


def _projection(x2d, w, *, name, bsz, seq, blk, tm, tn, out_dtype=BF16, rope=None, scale=None, perm=False,
                layout="rows", bias=None, emit_x=False):
    t, d = x2d.shape
    n = w.shape[1]
    per_seq = seq // tm
    grid = (t // tm, n // tn)
    in_specs = [pl.BlockSpec((tm, d), lambda i, j: (i, 0)),
                pl.BlockSpec((d, tn), lambda i, j: (0, j))]
    args = [x2d, w]
    if rope is not None:
        tab_spec = pl.BlockSpec((tm, DA_HEAD_DIM), lambda i, j: (i % per_seq, 0))
        in_specs += [tab_spec, tab_spec]
        args += list(rope)
    if bias is not None:
        in_specs.append(pl.BlockSpec((1, tn), lambda i, j: (0, j)))
        args.append(bias)
    if layout == "rows":
        out_shape = jax.ShapeDtypeStruct((t, n), out_dtype)
        out_spec = pl.BlockSpec((tm, tn), lambda i, j: (i, j))
    elif layout == "cols":
        out_shape = jax.ShapeDtypeStruct((bsz, n, seq), out_dtype)
        out_spec = pl.BlockSpec((1, tn, tm), lambda i, j: (i // per_seq, j, i % per_seq))
    else:
        out_shape = jax.ShapeDtypeStruct((bsz, n // HEAD_COLS, seq // blk, HEAD_COLS, blk), out_dtype)
        out_spec = pl.BlockSpec((1, tn // HEAD_COLS, tm // blk, HEAD_COLS, blk),
                                lambda i, j: (i // per_seq, j, i % per_seq, 0, 0))
    if emit_x:
        assert n == tn
        out_shape = (out_shape, jax.ShapeDtypeStruct((t, d), BF16))
        out_spec = (out_spec, pl.BlockSpec((tm, d), lambda i, j: (i, 0)))
    kern = functools.partial(_proj_kernel, rope=rope is not None, scale=scale, perm=perm, layout=layout,
                             gate=bias is not None, emit_x=emit_x, blk=blk)
    return pl.pallas_call(
        kern,
        out_shape=out_shape,
        grid=grid,
        in_specs=in_specs,
        out_specs=out_spec,
        compiler_params=pltpu.CompilerParams(
            dimension_semantics=("parallel", "arbitrary"),
            vmem_limit_bytes=_vmem_limit(56 * 1024 * 1024)),
        name=name,
    )(*args)


def _sublane_allmax(v):
    for sh in (4, 2, 1):
        v = jnp.maximum(v, pltpu.roll(v, sh, axis=0))
    return v


def _diff_parts(i, qt_ref, k_ref, vt_ref, m_sc, l_sc, acc_sc, s_sc, p_sc, a_sc, *, blk):
    dh = DA_HEAD_DIM
    groups = blk // SUBLANES

    def scores(j):
        start = pl.multiple_of(j * blk, blk)
        k = k_ref[0, pl.ds(start, blk), :]
        qt = qt_ref[0]
        return [jnp.dot(k[:, c * dh:(c + 1) * dh], qt[c * dh:(c + 1) * dh, :], preferred_element_type=F32)
                for c in range(2)]

    def put(ref, slot, pair):
        for c in range(2):
            ref[slot, c] = pair[c]

    def softmax(src, dst, masked):
        if masked:
            key = lax.broadcasted_iota(jnp.int32, (blk, blk), 0)
            qry = lax.broadcasted_iota(jnp.int32, (blk, blk), 1)
            allowed = (key // CHUNK) <= (qry // CHUNK)
        for c in range(2):
            s = s_sc[src, c]
            if masked:
                s = jnp.where(allowed, s, -jnp.inf)
            s3 = s.reshape(groups, SUBLANES, blk)
            m_prev = m_sc[c]
            m_new = jnp.maximum(m_prev, _sublane_allmax(jnp.max(s3, axis=0)))
            alpha = jnp.exp2(m_prev - m_new)
            m_sc[c] = m_new
            a_sc[dst, c] = alpha
            m_slab = jnp.concatenate([m_new] * PACK, axis=0)
            l_part = jnp.zeros((SUBLANES, blk), F32)
            for g in range(0, groups, PACK):
                rows = pl.ds(g * SUBLANES, PACK * SUBLANES)
                s_slab = s_sc[src, c, rows, :]
                if masked:
                    s_slab = jnp.where(allowed[g * SUBLANES:(g + PACK) * SUBLANES], s_slab, -jnp.inf)
                p = jnp.exp2(s_slab - m_slab)
                for h in range(PACK):
                    l_part = l_part + p[h * SUBLANES:(h + 1) * SUBLANES]
                p_sc[dst, c, rows, :] = p.astype(BF16)
            l_sc[c] = alpha * l_sc[c] + l_part

    def accumulate(j, slot):
        vt = vt_ref[0, 0, j]
        for c in range(2):
            pv = jnp.dot(vt, p_sc[slot, c], preferred_element_type=F32)
            acc_sc[c] = acc_sc[c] * a_sc[slot, c][0:1] + pv

    def init():
        m_sc[...] = jnp.full(m_sc.shape, -jnp.inf, F32)
        l_sc[...] = jnp.zeros(l_sc.shape, F32)
        acc_sc[...] = jnp.zeros(acc_sc.shape, F32)
        put(s_sc, 1, scores(i))

    start = (init,
             lambda: softmax(1, 0, True),
             lambda: put(s_sc, 0, scores(jnp.maximum(i - 1, 0))))

    def trip(kb, cur):
        return (lambda: put(s_sc, 1 - cur, scores(jnp.maximum(kb - 1, 0))),
                lambda: softmax(cur, 1 - cur, False),
                lambda: accumulate(kb + 1, cur))

    def normalize(lq1_ref, lk1_ref, lq2_ref, lk2_ref, g_ref, o_ref, lambda_init):
        lam = (jnp.exp(jnp.sum(lq1_ref[...] * lk1_ref[...], axis=-1, keepdims=True))
               - jnp.exp(jnp.sum(lq2_ref[...] * lk2_ref[...], axis=-1, keepdims=True))
               + lambda_init)
        l0 = jnp.sum(l_sc[0], axis=0, keepdims=True)
        l1 = jnp.sum(l_sc[1], axis=0, keepdims=True)
        o = (acc_sc[0] / l0 - lam * (acc_sc[1] / l1)).T
        o = o * lax.rsqrt(jnp.mean(o * o, axis=-1, keepdims=True) + LN_EPS)
        o = o * g_ref[...] * (1.0 - lambda_init)
        o_ref[0] = o.astype(o_ref.dtype)

    finish = (lambda *_: accumulate(0, i % 2), normalize)
    return start, trip, finish


def _sb_parts(i, qt_ref, k_ref, v_ref, acc_sc, carry_sc, z_sc, w_sc, piece_sc, *, blk):
    groups = blk // SUBLANES

    def scores(kb):
        start = pl.multiple_of(kb * blk, blk)
        return jnp.dot(k_ref[0, pl.ds(start, blk), :], qt_ref[0], preferred_element_type=F32)

    def weights(src, dst, masked):
        sub = lax.broadcasted_iota(jnp.int32, (SUBLANES, blk), 0)
        qpos = lax.broadcasted_iota(jnp.int32, (SUBLANES, blk), 1)
        run = jnp.zeros((SUBLANES, blk), F32)
        for a in reversed(range(groups)):
            rows = pl.ds(a * SUBLANES, SUBLANES)
            z = z_sc[src, rows, :]
            neg_abs = pltpu.bitcast(pltpu.bitcast(z, jnp.uint32) | jnp.uint32(0x80000000), F32)
            ls_pos = jnp.minimum(z, 0.0) - jnp.log(1.0 + jnp.exp2(neg_abs)) * LOG2E
            piece = ls_pos + run
            if masked:
                run = run + jnp.where(sub * groups + a < qpos, ls_pos - z, 0.0)
            else:
                run = piece - z
            piece_sc[rows, :] = piece
        incl = run
        for sh in (1, 2, 4):
            incl = incl + jnp.where(sub + sh < SUBLANES, pltpu.roll(incl, SUBLANES - sh, axis=0), 0.0)
        offs = incl - run + carry_sc[...]
        carry_sc[...] += jnp.broadcast_to(incl[0:1], (SUBLANES, blk))
        offs_slab = jnp.concatenate([offs] * PACK, axis=0)
        for g in range(0, groups, PACK):
            rows = pl.ds(g * SUBLANES, PACK * SUBLANES)
            w = jnp.exp2(piece_sc[rows, :] + offs_slab)
            if masked:
                r = lax.broadcasted_iota(jnp.int32, (PACK * SUBLANES, blk), 0) + g * SUBLANES
                qp = lax.broadcasted_iota(jnp.int32, (PACK * SUBLANES, blk), 1)
                w = jnp.where(_key_of_row(r, blk) < qp, w, 0.0)
            w_sc[dst, rows, :] = w.astype(BF16)

    def accumulate(kb, slot):
        start = pl.multiple_of(kb * blk, blk)
        acc_sc[...] += lax.dot_general(w_sc[slot], v_ref[0, pl.ds(start, blk), :], (((0,), (0,)), ((), ())),
                                       preferred_element_type=F32)

    def put_scores(slot, kb):
        z_sc[slot] = scores(kb)

    def init():
        acc_sc[...] = jnp.zeros(acc_sc.shape, F32)
        carry_sc[...] = jnp.zeros(carry_sc.shape, F32)
        put_scores(1, i)

    start = (init,
             lambda: weights(1, 0, True),
             lambda: put_scores(0, jnp.maximum(i - 1, 0)))

    def trip(kb, cur):
        return (lambda: put_scores(1 - cur, jnp.maximum(kb - 1, 0)),
                lambda: weights(cur, 1 - cur, False),
                lambda: accumulate(kb + 1, cur))

    def write_out(o_ref):
        o_ref[0] = acc_sc[...].astype(o_ref.dtype)

    finish = (lambda *_: accumulate(0, i % 2), write_out)
    return start, trip, finish


def _mixers_kernel(lq1_ref, lk1_ref, lq2_ref, lk2_ref, g_ref, qa_ref, ka_ref, vta_ref, qb_ref, kb_ref, vb_ref,
                   oa_ref, ob_ref, m_sc, l_sc, acca_sc, s_sc, p_sc, a_sc, accb_sc, carry_sc, z_sc, w_sc, piece_sc,
                   *, blk, lambda_init):
    i = pl.program_id(2)
    a_start, a_trip, a_finish = _diff_parts(i, qa_ref, ka_ref, vta_ref, m_sc, l_sc, acca_sc, s_sc, p_sc, a_sc,
                                            blk=blk)
    b_start, b_trip, b_finish = _sb_parts(i, qb_ref, kb_ref, vb_ref, accb_sc, carry_sc, z_sc, w_sc, piece_sc,
                                          blk=blk)
    for a_stage, b_stage in zip(a_start, b_start):
        a_stage()
        b_stage()

    def trip(kb, cur):
        a_scores, a_softmax, a_acc = a_trip(kb, cur)
        b_scores, b_weights, b_acc = b_trip(kb, cur)
        for stage in (a_scores, b_weights, a_acc, b_scores, a_softmax, b_acc):
            stage()

    def loop_body(t, carry):
        for cur in range(2):
            pl.when(t % 2 == cur)(functools.partial(trip, i - 1 - t, cur))
        return carry

    lax.fori_loop(0, i, loop_body, 0)
    for a_stage, b_stage in zip(a_finish, b_finish):
        a_stage(lq1_ref, lk1_ref, lq2_ref, lk2_ref, g_ref, oa_ref, lambda_init)
        b_stage(ob_ref)


def _mixers(qa_t, ka, va_t, qb_t, kb_p, vb_p, lam_params, norm_g, *, blk, lambda_init):
    b, width, s = qa_t.shape
    hw = HEAD_COLS
    n_heads = width // hw
    nblk = s // blk
    vec_spec = pl.BlockSpec((1, DA_HEAD_DIM), lambda bi, h, i: (0, 0))
    q_spec = pl.BlockSpec((1, hw, blk), lambda bi, h, i: (bi, h, i))
    k_spec = pl.BlockSpec((1, s, hw), lambda bi, h, i: (bi, 0, h), pipeline_mode=pl.Buffered(1))
    v_spec = pl.BlockSpec((1, 1, nblk, hw, blk), lambda bi, h, i: (bi, h, 0, 0, 0), pipeline_mode=pl.Buffered(1))
    o_spec = pl.BlockSpec((1, blk, hw), lambda bi, h, i: (bi, i, h))
    kern = functools.partial(_mixers_kernel, blk=blk, lambda_init=lambda_init)
    out = jax.ShapeDtypeStruct((b, s, width), BF16)
    return pl.pallas_call(
        kern,
        out_shape=(out, out),
        grid=(b, n_heads, nblk),
        in_specs=[vec_spec, vec_spec, vec_spec, vec_spec,
                  pl.BlockSpec((1, hw), lambda bi, h, i: (0, 0)),
                  q_spec, k_spec, v_spec, q_spec, k_spec, k_spec],
        out_specs=(o_spec, o_spec),
        scratch_shapes=[pltpu.VMEM((2, SUBLANES, blk), F32),
                        pltpu.VMEM((2, SUBLANES, blk), F32),
                        pltpu.VMEM((2, hw, blk), F32),
                        pltpu.VMEM((2, 2, blk, blk), F32),
                        pltpu.VMEM((2, 2, blk, blk), BF16),
                        pltpu.VMEM((2, 2, SUBLANES, blk), F32),
                        pltpu.VMEM((blk, hw), F32),
                        pltpu.VMEM((SUBLANES, blk), F32),
                        pltpu.VMEM((2, blk, blk), F32),
                        pltpu.VMEM((2, blk, blk), BF16),
                        pltpu.VMEM((blk, blk), F32)],
        compiler_params=pltpu.CompilerParams(
            dimension_semantics=("parallel", "parallel", "arbitrary"),
            vmem_limit_bytes=_vmem_limit(58 * 1024 * 1024)),
        name="mixers",
    )(*lam_params, norm_g, qa_t, ka, va_t, qb_t, kb_p, vb_p)


def _merge_kernel(ya_ref, yb_ref, g_ref, x_ref, wa_ref, wb_ref, wo_ref, lg_ref, lb_ref, o_ref, *, alpha):
    d = x_ref.shape[1]
    pa = jnp.dot(ya_ref[...], wa_ref[...], preferred_element_type=F32)
    pb = jnp.dot(yb_ref[...], wb_ref[...], preferred_element_type=F32)
    mixed = g_ref[:, :d] * pa + g_ref[:, d:] * pb
    z = jnp.dot(mixed.astype(BF16), wo_ref[...], preferred_element_type=F32)
    o_ref[...] = _layer_norm(alpha * x_ref[...] + z, lg_ref[...], lb_ref[...])


def _merge(ya, yb, gates, x2d, wa, wb, wo, ln_g, ln_b, *, tm, alpha):
    t, d = x2d.shape
    row = lambda width: pl.BlockSpec((tm, width), lambda i: (i, 0))
    return pl.pallas_call(
        functools.partial(_merge_kernel, alpha=alpha),
        out_shape=jax.ShapeDtypeStruct((t, d), F32),
        grid=(t // tm,),
        in_specs=[row(ya.shape[1]), row(yb.shape[1]), row(gates.shape[1]), row(d),
                  _const_spec(wa.shape), _const_spec(wb.shape), _const_spec(wo.shape),
                  _const_spec(ln_g.shape), _const_spec(ln_b.shape)],
        out_specs=row(d),
        compiler_params=pltpu.CompilerParams(
            dimension_semantics=("parallel",),
            vmem_limit_bytes=_vmem_limit(56 * 1024 * 1024)),
        name="merge_ln1",
    )(ya, yb, gates, x2d, wa, wb, wo, ln_g, ln_b)


def _cross_kernel(x_ref, wq_ref, km_ref, vm_ref, wo_ref, lg_ref, lb_ref, o_ref, *, alpha, n_heads):
    x = x_ref[...]
    d = x.shape[1]
    dh = d // n_heads
    q = jnp.dot(x.astype(BF16), wq_ref[...], preferred_element_type=F32).astype(BF16)
    km = km_ref[0]
    vm = vm_ref[0]
    outs = []
    for h in range(n_heads):
        s = lax.dot_general(q[:, h * dh:(h + 1) * dh], km[:, h * dh:(h + 1) * dh],
                            (((1,), (1,)), ((), ())), preferred_element_type=F32) * (dh ** -0.5)
        s = s - jnp.max(s, axis=-1, keepdims=True)
        p = jnp.exp(s)
        p = p / jnp.sum(p, axis=-1, keepdims=True)
        outs.append(jnp.dot(p.astype(BF16), vm[:, h * dh:(h + 1) * dh], preferred_element_type=F32))
    o = jnp.concatenate(outs, axis=-1).astype(BF16)
    c = jnp.dot(o, wo_ref[...], preferred_element_type=F32)
    o_ref[...] = _layer_norm(alpha * x + c, lg_ref[...], lb_ref[...])


def _cross_attention(x2d, wq, kmem, vmem, wo, ln_g, ln_b, *, tm, seq, alpha):
    t, d = x2d.shape
    m = kmem.shape[1]
    per_seq = seq // tm
    row = pl.BlockSpec((tm, d), lambda i: (i, 0))
    mem_spec = pl.BlockSpec((1, m, d), lambda i: (i // per_seq, 0, 0))
    return pl.pallas_call(
        functools.partial(_cross_kernel, alpha=alpha, n_heads=MEM_HEADS),
        out_shape=jax.ShapeDtypeStruct((t, d), F32),
        grid=(t // tm,),
        in_specs=[row, _const_spec(wq.shape), mem_spec, mem_spec, _const_spec(wo.shape),
                  _const_spec(ln_g.shape), _const_spec(ln_b.shape)],
        out_specs=row,
        compiler_params=pltpu.CompilerParams(
            dimension_semantics=("parallel",),
            vmem_limit_bytes=_vmem_limit(56 * 1024 * 1024)),
        name="cross_ln2",
    )(x2d, wq, kmem, vmem, wo, ln_g, ln_b)


def _swiglu_kernel(x_ref, wg_ref, wu_ref, wd_ref, lg_ref, lb_ref, o_ref, xb_sc, acc_sc, *, alpha):
    j = pl.program_id(1)

    @pl.when(j == 0)
    def _():
        xb_sc[...] = x_ref[...].astype(BF16)
        acc_sc[...] = jnp.zeros(acc_sc.shape, F32)

    xb = xb_sc[...]
    g = jnp.dot(xb, wg_ref[...], preferred_element_type=F32)
    u = jnp.dot(xb, wu_ref[...], preferred_element_type=F32)
    hmid = (g * jax.nn.sigmoid(g) * u).astype(BF16)
    acc_sc[...] += jnp.dot(hmid, wd_ref[...], preferred_element_type=F32)

    @pl.when(j == pl.num_programs(1) - 1)
    def _():
        o_ref[...] = _layer_norm(alpha * x_ref[...] + acc_sc[...], lg_ref[...], lb_ref[...])


def _swiglu(x2d, wg, wu, wd, ln_g, ln_b, *, tm, tf, alpha):
    t, d = x2d.shape
    dff = wg.shape[1]
    row = pl.BlockSpec((tm, d), lambda i, j: (i, 0))
    vec = pl.BlockSpec((1, d), lambda i, j: (0, 0))
    return pl.pallas_call(
        functools.partial(_swiglu_kernel, alpha=alpha),
        out_shape=jax.ShapeDtypeStruct((t, d), F32),
        grid=(t // tm, dff // tf),
        in_specs=[row,
                  pl.BlockSpec((d, tf), lambda i, j: (0, j)),
                  pl.BlockSpec((d, tf), lambda i, j: (0, j)),
                  pl.BlockSpec((tf, d), lambda i, j: (j, 0)),
                  vec, vec],
        out_specs=row,
        scratch_shapes=[pltpu.VMEM((tm, d), BF16), pltpu.VMEM((tm, d), F32)],
        compiler_params=pltpu.CompilerParams(
            dimension_semantics=("parallel", "arbitrary"),
            vmem_limit_bytes=_vmem_limit(56 * 1024 * 1024)),
        name="swiglu_ln3",
    )(x2d, wg, wu, wd, ln_g, ln_b)


def _rope_tables(seq):
    inv = ROPE_THETA ** (-jnp.arange(0, DA_HEAD_DIM, 2, dtype=F32) / DA_HEAD_DIM)
    ang = jnp.arange(seq, dtype=F32)[:, None] * inv[None, :]
    cos, sin = jnp.cos(ang), jnp.sin(ang)
    return jnp.concatenate([cos, cos], axis=-1), jnp.concatenate([-sin, sin], axis=-1)


def kernel(x, mem, w_in, b_gate, lam_q1, lam_k1, lam_q2, lam_k2, da_norm_g, w_proj_a, w_proj_b, w_out,
           ln1_g, ln1_b, w_mq, w_mk, w_mv, w_mo, ln2_g, ln2_b, w_gate, w_up, w_down, ln3_g, ln3_b):
    bsz, seq, d = x.shape
    depth = w_in.shape[0]
    t = bsz * seq
    da_width = DA_HEADS * 2 * DA_HEAD_DIM
    sb_width = SB_HEADS * SB_HEAD_DIM
    alpha = (2.0 * depth) ** 0.25
    blk = min(ATTN_BLK, seq)
    rope = _rope_tables(seq)
    tm_proj = min(1024, seq)
    tm_tok = min(256, seq)
    tm_ff = min(512, seq)
    row_vec = lambda p: p.reshape(1, -1).astype(F32)

    x2d = x.reshape(t, d)
    for l in range(depth):
        lambda_init = 0.8 - 0.6 * math.exp(-0.3 * l)
        w_l = w_in[l].astype(BF16)
        proj_any = functools.partial(_projection, bsz=bsz, seq=seq, blk=blk, tm=tm_proj, tn=1024)
        bounds = [0, da_width, 2 * da_width, 3 * da_width, 3 * da_width + sb_width,
                  3 * da_width + 2 * sb_width, 3 * da_width + 3 * sb_width, w_l.shape[1]]
        w_qa, w_ka, w_va, w_qb, w_kb, w_vb, w_g = [w_l[:, lo:hi] for lo, hi in zip(bounds[:-1], bounds[1:])]
        qa_t, xb = proj_any(x2d, w_qa, name="proj_qa", rope=rope, scale=DA_HEAD_DIM ** -0.5 * LOG2E,
                            layout="cols", emit_x=True)
        proj = functools.partial(proj_any, xb)
        ka = proj(w_ka, name="proj_ka", rope=rope)
        va_t = proj(w_va, name="proj_va", layout="blocked")
        qb_t = proj(w_qb, name="proj_qb", scale=SB_HEAD_DIM ** -0.5 * LOG2E, layout="cols")
        kb_p = proj(w_kb, name="proj_kb", perm=True)
        vb_p = proj(w_vb, name="proj_vb", perm=True)
        gates = proj(w_g, name="proj_gate", bias=row_vec(b_gate[l]))

        ya, yb = _mixers(qa_t, ka.reshape(bsz, seq, da_width), va_t,
                         qb_t, kb_p.reshape(bsz, seq, sb_width), vb_p.reshape(bsz, seq, sb_width),
                         [row_vec(p[l]) for p in (lam_q1, lam_k1, lam_q2, lam_k2)],
                         row_vec(da_norm_g[l]), blk=blk, lambda_init=lambda_init)

        x2d = _merge(ya.reshape(t, da_width), yb.reshape(t, sb_width), gates, x2d,
                     w_proj_a[l].astype(BF16), w_proj_b[l].astype(BF16), w_out[l].astype(BF16),
                     row_vec(ln1_g[l]), row_vec(ln1_b[l]), tm=tm_tok, alpha=alpha)
        n_mem = mem.shape[1]
        mem2d = mem.reshape(bsz * n_mem, d).astype(BF16)
        w_kv = jnp.concatenate([w_mk[l], w_mv[l]], axis=1).astype(BF16)
        kv = _projection(mem2d, w_kv, name="proj_mem", bsz=bsz, seq=n_mem, blk=blk,
                         tm=min(256, n_mem), tn=1024)
        kmem = kv[:, :d].reshape(bsz, n_mem, d)
        vmem = kv[:, d:].reshape(bsz, n_mem, d)
        x2d = _cross_attention(x2d, w_mq[l].astype(BF16), kmem, vmem, w_mo[l].astype(BF16),
                               row_vec(ln2_g[l]), row_vec(ln2_b[l]), tm=tm_tok, seq=seq, alpha=alpha)
        x2d = _swiglu(x2d, w_gate[l].astype(BF16), w_up[l].astype(BF16), w_down[l].astype(BF16),
                      row_vec(ln3_g[l]), row_vec(ln3_b[l]), tm=tm_ff, tf=512, alpha=alpha)
    return x2d.reshape(bsz, seq, d)
```

```python
import functools
import math

import jax
import jax.numpy as jnp
from jax import lax
from jax.experimental import pallas as pl
from jax.experimental.pallas import tpu as pltpu

F32 = jnp.float32
BF16 = jnp.bfloat16

CHUNK = 64
ROPE_THETA = 10000.0
LN_EPS = 1e-5
DA_HEADS = 4
DA_HEAD_DIM = 128
SB_HEADS = 4
SB_HEAD_DIM = 256
MEM_HEADS = 4
LOG2E = math.log2(math.e)

LANES = 128
SUBLANES = 8
PACK = 2
VMEM_BYTES = 64 * 1024 * 1024

ATTN_BLK = 512
HEAD_COLS = 256


def _vmem_limit(nbytes):
    return int(min(nbytes, VMEM_BYTES - 6 * 1024 * 1024))


def _layer_norm(r, g, b):
    mu = jnp.mean(r, axis=-1, keepdims=True)
    d = r - mu
    var = jnp.mean(d * d, axis=-1, keepdims=True)
    return d * lax.rsqrt(var + LN_EPS) * g + b


def _const_spec(shape):
    nd = len(shape)
    return pl.BlockSpec(shape, lambda *_: (0,) * nd, pipeline_mode=pl.Buffered(1))


def _key_of_row(r, blk):
    return (r % SUBLANES) * (blk // SUBLANES) + r // SUBLANES


def _proj_kernel(*refs, rope, scale, perm, layout, gate, emit_x, blk):
    it = iter(refs)
    x_ref, w_ref = next(it), next(it)
    cos_ref, sin_ref = (next(it), next(it)) if rope else (None, None)
    b_ref = next(it) if gate else None
    o_ref = next(it)
    tm = x_ref.shape[0]
    x = x_ref[...].astype(BF16)
    if emit_x:
        next(it)[...] = x
    if perm:
        r = lax.broadcasted_iota(jnp.int32, (blk, blk), 0)
        c = lax.broadcasted_iota(jnp.int32, (blk, blk), 1)
        pmat = (c == _key_of_row(r, blk)).astype(BF16)
        x = jnp.concatenate(
            [jnp.dot(pmat, x[s * blk:(s + 1) * blk], preferred_element_type=F32).astype(BF16)
             for s in range(tm // blk)], axis=0)
    acc = jnp.dot(x, w_ref[...], preferred_element_type=F32)
    n = acc.shape[1]
    if rope:
        cos = cos_ref[...]
        sin = sin_ref[...]
        heads = []
        for h in range(n // DA_HEAD_DIM):
            t = acc[:, h * DA_HEAD_DIM:(h + 1) * DA_HEAD_DIM]
            heads.append(t * cos + pltpu.roll(t, DA_HEAD_DIM // 2, axis=1) * sin)
        acc = jnp.concatenate(heads, axis=1)
    if scale is not None:
        acc = acc * scale
    if gate:
        acc = jax.nn.sigmoid(acc + b_ref[...])
    if layout == "rows":
        o_ref[...] = acc.astype(o_ref.dtype)
    else:
        o_ref[0] = acc.T.astype(o_ref.dtype)


def _projection(x2d, w, *, name, bsz, seq, blk, tm, tn, out_dtype=BF16, rope=None, scale=None, perm=False,
                layout="rows", bias=None, emit_x=False):
    t, d = x2d.shape
    n = w.shape[1]
    per_seq = seq // tm
    grid = (t // tm, n // tn)
    in_specs = [pl.BlockSpec((tm, d), lambda i, j: (i, 0)),
                pl.BlockSpec((d, tn), lambda i, j: (0, j))]
    args = [x2d, w]
    if rope is not None:
        tab_spec = pl.BlockSpec((tm, DA_HEAD_DIM), lambda i, j: (i % per_seq, 0))
        in_specs += [tab_spec, tab_spec]
        args += list(rope)
    if bias is not None:
        in_specs.append(pl.BlockSpec((1, tn), lambda i, j: (0, j)))
        args.append(bias)
    if layout == "rows":
        out_shape = jax.ShapeDtypeStruct((t, n), out_dtype)
        out_spec = pl.BlockSpec((tm, tn), lambda i, j: (i, j))
    else:
        assert layout == "cols"
        out_shape = jax.ShapeDtypeStruct((bsz, n, seq), out_dtype)
        out_spec = pl.BlockSpec((1, tn, tm), lambda i, j: (i // per_seq, j, i % per_seq))
    if emit_x:
        assert n == tn
        out_shape = (out_shape, jax.ShapeDtypeStruct((t, d), BF16))
        out_spec = (out_spec, pl.BlockSpec((tm, d), lambda i, j: (i, 0)))
    kern = functools.partial(_proj_kernel, rope=rope is not None, scale=scale, perm=perm, layout=layout,
                             gate=bias is not None, emit_x=emit_x, blk=blk)
    return pl.pallas_call(
        kern,
        out_shape=out_shape,
        grid=grid,
        in_specs=in_specs,
        out_specs=out_spec,
        compiler_params=pltpu.CompilerParams(
            dimension_semantics=("parallel", "arbitrary"),
            vmem_limit_bytes=_vmem_limit(56 * 1024 * 1024)),
        name=name,
    )(*args)


def _sublane_allmax(v):
    for sh in (4, 2, 1):
        v = jnp.maximum(v, pltpu.roll(v, sh, axis=0))
    return v


def _diff_parts(i, qt_ref, k_ref, v_ref, m_sc, l_sc, acc_sc, s_sc, p_sc, a_sc, *, blk):
    dh = DA_HEAD_DIM
    groups = blk // SUBLANES

    def scores(j):
        start = pl.multiple_of(j * blk, blk)
        k = k_ref[0, pl.ds(start, blk), :]
        qt = qt_ref[0]
        return [jnp.dot(k[:, c * dh:(c + 1) * dh], qt[c * dh:(c + 1) * dh, :], preferred_element_type=F32)
                for c in range(2)]

    def put(ref, slot, pair):
        for c in range(2):
            ref[slot, c] = pair[c]

    def softmax(src, dst, masked):
        if masked:
            key = lax.broadcasted_iota(jnp.int32, (blk, blk), 0)
            qry = lax.broadcasted_iota(jnp.int32, (blk, blk), 1)
            allowed = (key // CHUNK) <= (qry // CHUNK)
        for c in range(2):
            s = s_sc[src, c]
            if masked:
                s = jnp.where(allowed, s, -jnp.inf)
            s3 = s.reshape(groups, SUBLANES, blk)
            m_prev = m_sc[c]
            m_new = jnp.maximum(m_prev, _sublane_allmax(jnp.max(s3, axis=0)))
            alpha = jnp.exp2(m_prev - m_new)
            m_sc[c] = m_new
            a_sc[dst, c] = alpha
            m_slab = jnp.concatenate([m_new] * PACK, axis=0)
            l_part = jnp.zeros((SUBLANES, blk), F32)
            for g in range(0, groups, PACK):
                rows = pl.ds(g * SUBLANES, PACK * SUBLANES)
                s_slab = s_sc[src, c, rows, :]
                if masked:
                    s_slab = jnp.where(allowed[g * SUBLANES:(g + PACK) * SUBLANES], s_slab, -jnp.inf)
                p = jnp.exp2(s_slab - m_slab)
                for h in range(PACK):
                    l_part = l_part + p[h * SUBLANES:(h + 1) * SUBLANES]
                p_sc[dst, c, rows, :] = p.astype(BF16)
            l_sc[c] = alpha * l_sc[c] + l_part

    def column(row8):
        return row8.T[:, 0:1]

    def accumulate(j, slot):
        start = pl.multiple_of(j * blk, blk)
        v = v_ref[0, pl.ds(start, blk), :]
        for c in range(2):
            pv = lax.dot_general(p_sc[slot, c], v, (((0,), (0,)), ((), ())), preferred_element_type=F32)
            acc_sc[c] = acc_sc[c] * column(a_sc[slot, c]) + pv

    def init():
        m_sc[...] = jnp.full(m_sc.shape, -jnp.inf, F32)
        l_sc[...] = jnp.zeros(l_sc.shape, F32)
        acc_sc[...] = jnp.zeros(acc_sc.shape, F32)
        put(s_sc, 1, scores(i))

    start = (init,
             lambda: softmax(1, 0, True),
             lambda: put(s_sc, 0, scores(jnp.maximum(i - 1, 0))))

    def trip(kb, cur):
        return (lambda: put(s_sc, 1 - cur, scores(jnp.maximum(kb - 1, 0))),
                lambda: softmax(cur, 1 - cur, False),
                lambda: accumulate(kb + 1, cur))

    def normalize(lq1_ref, lk1_ref, lq2_ref, lk2_ref, g_ref, o_ref, lambda_init):
        lam = (jnp.exp(jnp.sum(lq1_ref[...] * lk1_ref[...], axis=-1, keepdims=True))
               - jnp.exp(jnp.sum(lq2_ref[...] * lk2_ref[...], axis=-1, keepdims=True))
               + lambda_init)
        l0 = column(jnp.broadcast_to(jnp.sum(l_sc[0], axis=0, keepdims=True), (SUBLANES, blk)))
        l1 = column(jnp.broadcast_to(jnp.sum(l_sc[1], axis=0, keepdims=True), (SUBLANES, blk)))
        o = acc_sc[0] / l0 - lam * (acc_sc[1] / l1)
        o = o * lax.rsqrt(jnp.mean(o * o, axis=-1, keepdims=True) + LN_EPS)
        o = o * g_ref[...] * (1.0 - lambda_init)
        o_ref[0] = o.astype(o_ref.dtype)

    finish = (lambda *_: accumulate(0, i % 2), normalize)
    return start, trip, finish


def _sb_parts(i, qt_ref, k_ref, v_ref, acc_sc, carry_sc, z_sc, w_sc, piece_sc, *, blk):
    groups = blk // SUBLANES

    def scores(kb):
        start = pl.multiple_of(kb * blk, blk)
        return jnp.dot(k_ref[0, pl.ds(start, blk), :], qt_ref[0], preferred_element_type=F32)

    def weights(src, dst, masked):
        sub = lax.broadcasted_iota(jnp.int32, (SUBLANES, blk), 0)
        qpos = lax.broadcasted_iota(jnp.int32, (SUBLANES, blk), 1)
        run = jnp.zeros((SUBLANES, blk), F32)
        for a in reversed(range(groups)):
            rows = pl.ds(a * SUBLANES, SUBLANES)
            z = z_sc[src, rows, :]
            neg_abs = pltpu.bitcast(pltpu.bitcast(z, jnp.uint32) | jnp.uint32(0x80000000), F32)
            ls_pos = jnp.minimum(z, 0.0) - jnp.log(1.0 + jnp.exp2(neg_abs)) * LOG2E
            piece = ls_pos + run
            if masked:
                run = run + jnp.where(sub * groups + a < qpos, ls_pos - z, 0.0)
            else:
                run = piece - z
            piece_sc[rows, :] = piece
        incl = run
        for sh in (1, 2, 4):
            incl = incl + jnp.where(sub + sh < SUBLANES, pltpu.roll(incl, SUBLANES - sh, axis=0), 0.0)
        offs = incl - run + carry_sc[...]
        carry_sc[...] += jnp.broadcast_to(incl[0:1], (SUBLANES, blk))
        offs_slab = jnp.concatenate([offs] * PACK, axis=0)
        for g in range(0, groups, PACK):
            rows = pl.ds(g * SUBLANES, PACK * SUBLANES)
            w = jnp.exp2(piece_sc[rows, :] + offs_slab)
            if masked:
                r = lax.broadcasted_iota(jnp.int32, (PACK * SUBLANES, blk), 0) + g * SUBLANES
                qp = lax.broadcasted_iota(jnp.int32, (PACK * SUBLANES, blk), 1)
                w = jnp.where(_key_of_row(r, blk) < qp, w, 0.0)
            w_sc[dst, rows, :] = w.astype(BF16)

    def accumulate(kb, slot):
        start = pl.multiple_of(kb * blk, blk)
        acc_sc[...] += lax.dot_general(w_sc[slot], v_ref[0, pl.ds(start, blk), :], (((0,), (0,)), ((), ())),
                                       preferred_element_type=F32)

    def put_scores(slot, kb):
        z_sc[slot] = scores(kb)

    def init():
        acc_sc[...] = jnp.zeros(acc_sc.shape, F32)
        carry_sc[...] = jnp.zeros(carry_sc.shape, F32)
        put_scores(1, i)

    start = (init,
             lambda: weights(1, 0, True),
             lambda: put_scores(0, jnp.maximum(i - 1, 0)))

    def trip(kb, cur):
        return (lambda: put_scores(1 - cur, jnp.maximum(kb - 1, 0)),
                lambda: weights(cur, 1 - cur, False),
                lambda: accumulate(kb + 1, cur))

    def write_out(o_ref):
        o_ref[0] = acc_sc[...].astype(o_ref.dtype)

    finish = (lambda *_: accumulate(0, i % 2), write_out)
    return start, trip, finish


def _mixers_kernel(lq1_ref, lk1_ref, lq2_ref, lk2_ref, g_ref, qa_ref, ka_ref, va_ref, qb_ref, kb_ref, vb_ref,
                   oa_ref, ob_ref, m_sc, l_sc, acca_sc, s_sc, p_sc, a_sc, accb_sc, carry_sc, z_sc, w_sc, piece_sc,
                   *, blk, lambda_init):
    i = pl.program_id(2)
    a_start, a_trip, a_finish = _diff_parts(i, qa_ref, ka_ref, va_ref, m_sc, l_sc, acca_sc, s_sc, p_sc, a_sc,
                                            blk=blk)
    b_start, b_trip, b_finish = _sb_parts(i, qb_ref, kb_ref, vb_ref, accb_sc, carry_sc, z_sc, w_sc, piece_sc,
                                          blk=blk)
    for a_stage, b_stage in zip(a_start, b_start):
        a_stage()
        b_stage()

    def trip(kb, cur):
        a_scores, a_softmax, a_acc = a_trip(kb, cur)
        b_scores, b_weights, b_acc = b_trip(kb, cur)
        for stage in (a_scores, b_weights, a_acc, b_scores, a_softmax, b_acc):
            stage()

    def loop_body(t, carry):
        for cur in range(2):
            pl.when(t % 2 == cur)(functools.partial(trip, i - 1 - t, cur))
        return carry

    lax.fori_loop(0, i, loop_body, 0)
    for a_stage, b_stage in zip(a_finish, b_finish):
        a_stage(lq1_ref, lk1_ref, lq2_ref, lk2_ref, g_ref, oa_ref, lambda_init)
        b_stage(ob_ref)


def _mixers(qa_t, ka, va, qb_t, kb_p, vb_p, lam_params, norm_g, *, blk, lambda_init):
    b, width, s = qa_t.shape
    hw = HEAD_COLS
    n_heads = width // hw
    nblk = s // blk
    vec_spec = pl.BlockSpec((1, DA_HEAD_DIM), lambda bi, h, i: (0, 0))
    q_spec = pl.BlockSpec((1, hw, blk), lambda bi, h, i: (bi, h, i))
    k_spec = pl.BlockSpec((1, s, hw), lambda bi, h, i: (bi, 0, h), pipeline_mode=pl.Buffered(1))
    o_spec = pl.BlockSpec((1, blk, hw), lambda bi, h, i: (bi, i, h))
    kern = functools.partial(_mixers_kernel, blk=blk, lambda_init=lambda_init)
    out = jax.ShapeDtypeStruct((b, s, width), BF16)
    return pl.pallas_call(
        kern,
        out_shape=(out, out),
        grid=(b, n_heads, nblk),
        in_specs=[vec_spec, vec_spec, vec_spec, vec_spec,
                  pl.BlockSpec((1, hw), lambda bi, h, i: (0, 0)),
                  q_spec, k_spec, k_spec, q_spec, k_spec, k_spec],
        out_specs=(o_spec, o_spec),
        scratch_shapes=[pltpu.VMEM((2, SUBLANES, blk), F32),
                        pltpu.VMEM((2, SUBLANES, blk), F32),
                        pltpu.VMEM((2, blk, hw), F32),
                        pltpu.VMEM((2, 2, blk, blk), F32),
                        pltpu.VMEM((2, 2, blk, blk), BF16),
                        pltpu.VMEM((2, 2, SUBLANES, blk), F32),
                        pltpu.VMEM((blk, hw), F32),
                        pltpu.VMEM((SUBLANES, blk), F32),
                        pltpu.VMEM((2, blk, blk), F32),
                        pltpu.VMEM((2, blk, blk), BF16),
                        pltpu.VMEM((blk, blk), F32)],
        compiler_params=pltpu.CompilerParams(
            dimension_semantics=("parallel", "parallel", "arbitrary"),
            vmem_limit_bytes=_vmem_limit(58 * 1024 * 1024)),
        name="mixers",
    )(*lam_params, norm_g, qa_t, ka, va, qb_t, kb_p, vb_p)


def _merge_kernel(ya_ref, yb_ref, g_ref, x_ref, wa_ref, wb_ref, wo_ref, lg_ref, lb_ref, o_ref, *, alpha):
    d = x_ref.shape[1]
    pa = jnp.dot(ya_ref[...], wa_ref[...], preferred_element_type=F32)
    pb = jnp.dot(yb_ref[...], wb_ref[...], preferred_element_type=F32)
    mixed = g_ref[:, :d] * pa + g_ref[:, d:] * pb
    z = jnp.dot(mixed.astype(BF16), wo_ref[...], preferred_element_type=F32)
    o_ref[...] = _layer_norm(alpha * x_ref[...] + z, lg_ref[...], lb_ref[...])


def _merge(ya, yb, gates, x2d, wa, wb, wo, ln_g, ln_b, *, tm, alpha):
    t, d = x2d.shape
    row = lambda width: pl.BlockSpec((tm, width), lambda i: (i, 0))
    return pl.pallas_call(
        functools.partial(_merge_kernel, alpha=alpha),
        out_shape=jax.ShapeDtypeStruct((t, d), F32),
        grid=(t // tm,),
        in_specs=[row(ya.shape[1]), row(yb.shape[1]), row(gates.shape[1]), row(d),
                  _const_spec(wa.shape), _const_spec(wb.shape), _const_spec(wo.shape),
                  _const_spec(ln_g.shape), _const_spec(ln_b.shape)],
        out_specs=row(d),
        compiler_params=pltpu.CompilerParams(
            dimension_semantics=("parallel",),
            vmem_limit_bytes=_vmem_limit(56 * 1024 * 1024)),
        name="merge_ln1",
    )(ya, yb, gates, x2d, wa, wb, wo, ln_g, ln_b)


def _cross_kernel(x_ref, wq_ref, km_ref, vm_ref, wo_ref, lg_ref, lb_ref, o_ref, *, alpha, n_heads):
    x = x_ref[...]
    d = x.shape[1]
    dh = d // n_heads
    q = jnp.dot(x.astype(BF16), wq_ref[...], preferred_element_type=F32).astype(BF16)
    km = km_ref[0]
    vm = vm_ref[0]
    outs = []
    for h in range(n_heads):
        s = lax.dot_general(q[:, h * dh:(h + 1) * dh], km[:, h * dh:(h + 1) * dh],
                            (((1,), (1,)), ((), ())), preferred_element_type=F32) * (dh ** -0.5)
        s = s - jnp.max(s, axis=-1, keepdims=True)
        p = jnp.exp(s)
        p = p / jnp.sum(p, axis=-1, keepdims=True)
        outs.append(jnp.dot(p.astype(BF16), vm[:, h * dh:(h + 1) * dh], preferred_element_type=F32))
    o = jnp.concatenate(outs, axis=-1).astype(BF16)
    c = jnp.dot(o, wo_ref[...], preferred_element_type=F32)
    o_ref[...] = _layer_norm(alpha * x + c, lg_ref[...], lb_ref[...])


def _cross_attention(x2d, wq, kmem, vmem, wo, ln_g, ln_b, *, tm, seq, alpha):
    t, d = x2d.shape
    m = kmem.shape[1]
    per_seq = seq // tm
    row = pl.BlockSpec((tm, d), lambda i: (i, 0))
    mem_spec = pl.BlockSpec((1, m, d), lambda i: (i // per_seq, 0, 0))
    return pl.pallas_call(
        functools.partial(_cross_kernel, alpha=alpha, n_heads=MEM_HEADS),
        out_shape=jax.ShapeDtypeStruct((t, d), F32),
        grid=(t // tm,),
        in_specs=[row, _const_spec(wq.shape), mem_spec, mem_spec, _const_spec(wo.shape),
                  _const_spec(ln_g.shape), _const_spec(ln_b.shape)],
        out_specs=row,
        compiler_params=pltpu.CompilerParams(
            dimension_semantics=("parallel",),
            vmem_limit_bytes=_vmem_limit(56 * 1024 * 1024)),
        name="cross_ln2",
    )(x2d, wq, kmem, vmem, wo, ln_g, ln_b)


def _swiglu_kernel(x_ref, wg_ref, wu_ref, wd_ref, lg_ref, lb_ref, o_ref, xb_sc, acc_sc, *, alpha):
    j = pl.program_id(1)

    @pl.when(j == 0)
    def _():
        xb_sc[...] = x_ref[...].astype(BF16)
        acc_sc[...] = jnp.zeros(acc_sc.shape, F32)

    xb = xb_sc[...]
    g = jnp.dot(xb, wg_ref[...], preferred_element_type=F32)
    u = jnp.dot(xb, wu_ref[...], preferred_element_type=F32)
    hmid = (g * jax.nn.sigmoid(g) * u).astype(BF16)
    acc_sc[...] += jnp.dot(hmid, wd_ref[...], preferred_element_type=F32)

    @pl.when(j == pl.num_programs(1) - 1)
    def _():
        o_ref[...] = _layer_norm(alpha * x_ref[...] + acc_sc[...], lg_ref[...], lb_ref[...])


def _swiglu(x2d, wg, wu, wd, ln_g, ln_b, *, tm, tf, alpha):
    t, d = x2d.shape
    dff = wg.shape[1]
    row = pl.BlockSpec((tm, d), lambda i, j: (i, 0))
    vec = pl.BlockSpec((1, d), lambda i, j: (0, 0))
    return pl.pallas_call(
        functools.partial(_swiglu_kernel, alpha=alpha),
        out_shape=jax.ShapeDtypeStruct((t, d), F32),
        grid=(t // tm, dff // tf),
        in_specs=[row,
                  pl.BlockSpec((d, tf), lambda i, j: (0, j)),
                  pl.BlockSpec((d, tf), lambda i, j: (0, j)),
                  pl.BlockSpec((tf, d), lambda i, j: (j, 0)),
                  vec, vec],
        out_specs=row,
        scratch_shapes=[pltpu.VMEM((tm, d), BF16), pltpu.VMEM((tm, d), F32)],
        compiler_params=pltpu.CompilerParams(
            dimension_semantics=("parallel", "arbitrary"),
            vmem_limit_bytes=_vmem_limit(56 * 1024 * 1024)),
        name="swiglu_ln3",
    )(x2d, wg, wu, wd, ln_g, ln_b)


def _rope_tables(seq):
    inv = ROPE_THETA ** (-jnp.arange(0, DA_HEAD_DIM, 2, dtype=F32) / DA_HEAD_DIM)
    ang = jnp.arange(seq, dtype=F32)[:, None] * inv[None, :]
    cos, sin = jnp.cos(ang), jnp.sin(ang)
    return jnp.concatenate([cos, cos], axis=-1), jnp.concatenate([-sin, sin], axis=-1)


def kernel(x, mem, w_in, b_gate, lam_q1, lam_k1, lam_q2, lam_k2, da_norm_g, w_proj_a, w_proj_b, w_out,
           ln1_g, ln1_b, w_mq, w_mk, w_mv, w_mo, ln2_g, ln2_b, w_gate, w_up, w_down, ln3_g, ln3_b):
    bsz, seq, d = x.shape
    depth = w_in.shape[0]
    t = bsz * seq
    da_width = DA_HEADS * 2 * DA_HEAD_DIM
    sb_width = SB_HEADS * SB_HEAD_DIM
    alpha = (2.0 * depth) ** 0.25
    blk = min(ATTN_BLK, seq)
    rope = _rope_tables(seq)
    tm_proj = min(1024, seq)
    tm_tok = min(256, seq)
    tm_ff = min(512, seq)
    row_vec = lambda p: p.reshape(1, -1).astype(F32)

    x2d = x.reshape(t, d)
    for l in range(depth):
        lambda_init = 0.8 - 0.6 * math.exp(-0.3 * l)
        w_l = w_in[l].astype(BF16)
        proj_any = functools.partial(_projection, bsz=bsz, seq=seq, blk=blk, tm=tm_proj, tn=1024)
        bounds = [0, da_width, 2 * da_width, 3 * da_width, 3 * da_width + sb_width,
                  3 * da_width + 2 * sb_width, 3 * da_width + 3 * sb_width, w_l.shape[1]]
        w_qa, w_ka, w_va, w_qb, w_kb, w_vb, w_g = [w_l[:, lo:hi] for lo, hi in zip(bounds[:-1], bounds[1:])]
        qa_t, xb = proj_any(x2d, w_qa, name="proj_qa", rope=rope, scale=DA_HEAD_DIM ** -0.5 * LOG2E,
                            layout="cols", emit_x=True)
        proj = functools.partial(proj_any, xb)
        ka = proj(w_ka, name="proj_ka", rope=rope)
        va = proj(w_va, name="proj_va")
        qb_t = proj(w_qb, name="proj_qb", scale=SB_HEAD_DIM ** -0.5 * LOG2E, layout="cols")
        kb_p = proj(w_kb, name="proj_kb", perm=True)
        vb_p = proj(w_vb, name="proj_vb", perm=True)
        gates = proj(w_g, name="proj_gate", bias=row_vec(b_gate[l]))

        ya, yb = _mixers(qa_t, ka.reshape(bsz, seq, da_width), va.reshape(bsz, seq, da_width),
                         qb_t, kb_p.reshape(bsz, seq, sb_width), vb_p.reshape(bsz, seq, sb_width),
                         [row_vec(p[l]) for p in (lam_q1, lam_k1, lam_q2, lam_k2)],
                         row_vec(da_norm_g[l]), blk=blk, lambda_init=lambda_init)

        x2d = _merge(ya.reshape(t, da_width), yb.reshape(t, sb_width), gates, x2d,
                     w_proj_a[l].astype(BF16), w_proj_b[l].astype(BF16), w_out[l].astype(BF16),
                     row_vec(ln1_g[l]), row_vec(ln1_b[l]), tm=tm_tok, alpha=alpha)
        n_mem = mem.shape[1]
        mem2d = mem.reshape(bsz * n_mem, d).astype(BF16)
        w_kv = jnp.concatenate([w_mk[l], w_mv[l]], axis=1).astype(BF16)
        kv = _projection(mem2d, w_kv, name="proj_mem", bsz=bsz, seq=n_mem, blk=blk,
                         tm=min(256, n_mem), tn=1024)
        kmem = kv[:, :d].reshape(bsz, n_mem, d)
        vmem = kv[:, d:].reshape(bsz, n_mem, d)
        x2d = _cross_attention(x2d, w_mq[l].astype(BF16), kmem, vmem, w_mo[l].astype(BF16),
                               row_vec(ln2_g[l]), row_vec(ln2_b[l]), tm=tm_tok, seq=seq, alpha=alpha)
        x2d = _swiglu(x2d, w_gate[l].astype(BF16), w_up[l].astype(BF16), w_down[l].astype(BF16),
                      row_vec(ln3_g[l]), row_vec(ln3_b[l]), tm=tm_ff, tf=512, alpha=alpha)
    return x2d.reshape(bsz, seq, d)
```

```python
import functools
import math

import jax
import jax.numpy as jnp
from jax import lax
from jax.experimental import pallas as pl
from jax.experimental.pallas import tpu as pltpu

F32 = jnp.float32
BF16 = jnp.bfloat16

CHUNK = 64
ROPE_THETA = 10000.0
LN_EPS = 1e-5
DA_HEADS = 4
DA_HEAD_DIM = 128
SB_HEADS = 4
SB_HEAD_DIM = 256
MEM_HEADS = 4
LOG2E = math.log2(math.e)

LANES = 128
SUBLANES = 8
PACK = 2
VMEM_BYTES = 64 * 1024 * 1024

ATTN_BLK = 512
HEAD_COLS = 256


def _vmem_limit(nbytes):
    return int(min(nbytes, VMEM_BYTES - 6 * 1024 * 1024))


def _layer_norm(r, g, b):
    mu = jnp.mean(r, axis=-1, keepdims=True)
    d = r - mu
    var = jnp.mean(d * d, axis=-1, keepdims=True)
    return d * lax.rsqrt(var + LN_EPS) * g + b


def _const_spec(shape):
    nd = len(shape)
    return pl.BlockSpec(shape, lambda *_: (0,) * nd, pipeline_mode=pl.Buffered(1))


def _key_of_row(r, blk):
    return (r % SUBLANES) * (blk // SUBLANES) + r // SUBLANES


def _proj_kernel(*refs, rope, scale, perm, layout, gate, emit_x, blk):
    it = iter(refs)
    x_ref, w_ref = next(it), next(it)
    cos_ref, sin_ref = (next(it), next(it)) if rope else (None, None)
    b_ref = next(it) if gate else None
    o_ref = next(it)
    tm = x_ref.shape[0]
    x = x_ref[...].astype(BF16)
    if emit_x:
        next(it)[...] = x
    if perm:
        r = lax.broadcasted_iota(jnp.int32, (blk, blk), 0)
        c = lax.broadcasted_iota(jnp.int32, (blk, blk), 1)
        pmat = (c == _key_of_row(r, blk)).astype(BF16)
        x = jnp.concatenate(
            [jnp.dot(pmat, x[s * blk:(s + 1) * blk], preferred_element_type=F32).astype(BF16)
             for s in range(tm // blk)], axis=0)
    acc = jnp.dot(x, w_ref[...], preferred_element_type=F32)
    n = acc.shape[1]
    if rope:
        cos = cos_ref[...]
        sin = sin_ref[...]
        heads = []
        for h in range(n // DA_HEAD_DIM):
            t = acc[:, h * DA_HEAD_DIM:(h + 1) * DA_HEAD_DIM]
            heads.append(t * cos + pltpu.roll(t, DA_HEAD_DIM // 2, axis=1) * sin)
        acc = jnp.concatenate(heads, axis=1)
    if scale is not None:
        acc = acc * scale
    if gate:
        acc = jax.nn.sigmoid(acc + b_ref[...])
    if layout == "rows":
        o_ref[...] = acc.astype(o_ref.dtype)
    elif layout == "cols":
        o_ref[0] = acc.T.astype(o_ref.dtype)
    else:
        for h in range(n // HEAD_COLS):
            for s in range(tm // blk):
                tile = acc[s * blk:(s + 1) * blk, h * HEAD_COLS:(h + 1) * HEAD_COLS]
                o_ref[0, h, s] = tile.T.astype(o_ref.dtype)


def _projection(x2d, w, *, name, bsz, seq, blk, tm, tn, out_dtype=BF16, rope=None, scale=None, perm=False,
                layout="rows", bias=None, emit_x=False):
    t, d = x2d.shape
    n = w.shape[1]
    per_seq = seq // tm
    grid = (t // tm, n // tn)
    in_specs = [pl.BlockSpec((tm, d), lambda i, j: (i, 0)),
                pl.BlockSpec((d, tn), lambda i, j: (0, j))]
    args = [x2d, w]
    if rope is not None:
        tab_spec = pl.BlockSpec((tm, DA_HEAD_DIM), lambda i, j: (i % per_seq, 0))
        in_specs += [tab_spec, tab_spec]
        args += list(rope)
    if bias is not None:
        in_specs.append(pl.BlockSpec((1, tn), lambda i, j: (0, j)))
        args.append(bias)
    if layout == "rows":
        out_shape = jax.ShapeDtypeStruct((t, n), out_dtype)
        out_spec = pl.BlockSpec((tm, tn), lambda i, j: (i, j))
    elif layout == "cols":
        out_shape = jax.ShapeDtypeStruct((bsz, n, seq), out_dtype)
        out_spec = pl.BlockSpec((1, tn, tm), lambda i, j: (i // per_seq, j, i % per_seq))
    else:
        out_shape = jax.ShapeDtypeStruct((bsz, n // HEAD_COLS, seq // blk, HEAD_COLS, blk), out_dtype)
        out_spec = pl.BlockSpec((1, tn // HEAD_COLS, tm // blk, HEAD_COLS, blk),
                                lambda i, j: (i // per_seq, j, i % per_seq, 0, 0))
    if emit_x:
        assert n == tn
        out_shape = (out_shape, jax.ShapeDtypeStruct((t, d), BF16))
        out_spec = (out_spec, pl.BlockSpec((tm, d), lambda i, j: (i, 0)))
    kern = functools.partial(_proj_kernel, rope=rope is not None, scale=scale, perm=perm, layout=layout,
                             gate=bias is not None, emit_x=emit_x, blk=blk)
    return pl.pallas_call(
        kern,
        out_shape=out_shape,
        grid=grid,
        in_specs=in_specs,
        out_specs=out_spec,
        compiler_params=pltpu.CompilerParams(
            dimension_semantics=("parallel", "arbitrary"),
            vmem_limit_bytes=_vmem_limit(56 * 1024 * 1024)),
        name=name,
    )(*args)


def _sublane_allmax(v):
    for sh in (4, 2, 1):
        v = jnp.maximum(v, pltpu.roll(v, sh, axis=0))
    return v


def _diff_parts(i, qt_ref, k_ref, vt_ref, m_sc, l_sc, acc_sc, s_sc, p_sc, a_sc, *, blk):
    dh = DA_HEAD_DIM
    groups = blk // SUBLANES

    def scores(j):
        start = pl.multiple_of(j * blk, blk)
        k = k_ref[0, pl.ds(start, blk), :]
        qt = qt_ref[0]
        return [jnp.dot(k[:, c * dh:(c + 1) * dh], qt[c * dh:(c + 1) * dh, :], preferred_element_type=F32)
                for c in range(2)]

    def put(ref, slot, pair):
        for c in range(2):
            ref[slot, c] = pair[c]

    def softmax(src, dst, masked):
        if masked:
            key = lax.broadcasted_iota(jnp.int32, (blk, blk), 0)
            qry = lax.broadcasted_iota(jnp.int32, (blk, blk), 1)
            allowed = (key // CHUNK) <= (qry // CHUNK)
        for c in range(2):
            s = s_sc[src, c]
            if masked:
                s = jnp.where(allowed, s, -jnp.inf)
            s3 = s.reshape(groups, SUBLANES, blk)
            m_prev = m_sc[c]
            m_new = jnp.maximum(m_prev, _sublane_allmax(jnp.max(s3, axis=0)))
            alpha = jnp.exp2(m_prev - m_new)
            m_sc[c] = m_new
            a_sc[dst, c] = alpha
            m_slab = jnp.concatenate([m_new] * PACK, axis=0)
            l_part = jnp.zeros((SUBLANES, blk), F32)
            for g in range(0, groups, PACK):
                rows = pl.ds(g * SUBLANES, PACK * SUBLANES)
                s_slab = s_sc[src, c, rows, :]
                if masked:
                    s_slab = jnp.where(allowed[g * SUBLANES:(g + PACK) * SUBLANES], s_slab, -jnp.inf)
                p = jnp.exp2(s_slab - m_slab)
                for h in range(PACK):
                    l_part = l_part + p[h * SUBLANES:(h + 1) * SUBLANES]
                p_sc[dst, c, rows, :] = p.astype(BF16)
            l_sc[c] = alpha * l_sc[c] + l_part

    def accumulate(j, slot):
        vt = vt_ref[0, 0, j]
        for c in range(2):
            pv = jnp.dot(vt, p_sc[slot, c], preferred_element_type=F32)
            acc_sc[c] = acc_sc[c] * a_sc[slot, c][0:1] + pv

    def init():
        m_sc[...] = jnp.full(m_sc.shape, -jnp.inf, F32)
        l_sc[...] = jnp.zeros(l_sc.shape, F32)
        acc_sc[...] = jnp.zeros(acc_sc.shape, F32)
        put(s_sc, 1, scores(i))

    start = (init,
             lambda: softmax(1, 0, True),
             lambda: put(s_sc, 0, scores(jnp.maximum(i - 1, 0))))

    def trip(kb, cur):
        return (lambda: put(s_sc, 1 - cur, scores(jnp.maximum(kb - 1, 0))),
                lambda: softmax(cur, 1 - cur, False),
                lambda: accumulate(kb + 1, cur))

    def normalize(lq1_ref, lk1_ref, lq2_ref, lk2_ref, g_ref, o_ref, lambda_init):
        lam = (jnp.exp(jnp.sum(lq1_ref[...] * lk1_ref[...], axis=-1, keepdims=True))
               - jnp.exp(jnp.sum(lq2_ref[...] * lk2_ref[...], axis=-1, keepdims=True))
               + lambda_init)
        l0 = jnp.sum(l_sc[0], axis=0, keepdims=True)
        l1 = jnp.sum(l_sc[1], axis=0, keepdims=True)
        o = (acc_sc[0] / l0 - lam * (acc_sc[1] / l1)).T
        o = o * lax.rsqrt(jnp.mean(o * o, axis=-1, keepdims=True) + LN_EPS)
        o = o * g_ref[...] * (1.0 - lambda_init)
        o_ref[0] = o.astype(o_ref.dtype)

    finish = (lambda *_: accumulate(0, i % 2), normalize)
    return start, trip, finish


def _sb_parts(i, qt_ref, k_ref, v_ref, acc_sc, carry_sc, z_sc, w_sc, piece_sc, *, blk):
    groups = blk // SUBLANES

    def scores(kb):
        start = pl.multiple_of(kb * blk, blk)
        return jnp.dot(k_ref[0, pl.ds(start, blk), :], qt_ref[0], preferred_element_type=F32)

    def weights(src, dst, masked):
        sub = lax.broadcasted_iota(jnp.int32, (SUBLANES, blk), 0)
        qpos = lax.broadcasted_iota(jnp.int32, (SUBLANES, blk), 1)
        run = jnp.zeros((SUBLANES, blk), F32)
        for a in reversed(range(groups)):
            rows = pl.ds(a * SUBLANES, SUBLANES)
            z = z_sc[src, rows, :]
            neg_abs = pltpu.bitcast(pltpu.bitcast(z, jnp.uint32) | jnp.uint32(0x80000000), F32)
            ls_pos = jnp.minimum(z, 0.0) - jnp.log(1.0 + jnp.exp2(neg_abs)) * LOG2E
            piece = ls_pos + run
            if masked:
                run = run + jnp.where(sub * groups + a < qpos, ls_pos - z, 0.0)
            else:
                run = piece - z
            piece_sc[rows, :] = piece
        incl = run
        for sh in (1, 2, 4):
            incl = incl + jnp.where(sub + sh < SUBLANES, pltpu.roll(incl, SUBLANES - sh, axis=0), 0.0)
        offs = incl - run + carry_sc[...]
        carry_sc[...] += jnp.broadcast_to(incl[0:1], (SUBLANES, blk))
        offs_slab = jnp.concatenate([offs] * PACK, axis=0)
        for g in range(0, groups, PACK):
            rows = pl.ds(g * SUBLANES, PACK * SUBLANES)
            w = jnp.exp2(piece_sc[rows, :] + offs_slab)
            if masked:
                r = lax.broadcasted_iota(jnp.int32, (PACK * SUBLANES, blk), 0) + g * SUBLANES
                qp = lax.broadcasted_iota(jnp.int32, (PACK * SUBLANES, blk), 1)
                w = jnp.where(_key_of_row(r, blk) < qp, w, 0.0)
            w_sc[dst, rows, :] = w.astype(BF16)

    def accumulate(kb, slot):
        start = pl.multiple_of(kb * blk, blk)
        acc_sc[...] += lax.dot_general(w_sc[slot], v_ref[0, pl.ds(start, blk), :], (((0,), (0,)), ((), ())),
                                       preferred_element_type=F32)

    def put_scores(slot, kb):
        z_sc[slot] = scores(kb)

    def init():
        acc_sc[...] = jnp.zeros(acc_sc.shape, F32)
        carry_sc[...] = jnp.zeros(carry_sc.shape, F32)
        put_scores(1, i)

    start = (init,
             lambda: weights(1, 0, True),
             lambda: put_scores(0, jnp.maximum(i - 1, 0)))

    def trip(kb, cur):
        return (lambda: put_scores(1 - cur, jnp.maximum(kb - 1, 0)),
                lambda: weights(cur, 1 - cur, False),
                lambda: accumulate(kb + 1, cur))

    def write_out(o_ref):
        o_ref[0] = acc_sc[...].astype(o_ref.dtype)

    finish = (lambda *_: accumulate(0, i % 2), write_out)
    return start, trip, finish


def _mixers_kernel(lq1_ref, lk1_ref, lq2_ref, lk2_ref, g_ref, qa_ref, ka_ref, vta_ref, qb_ref, kb_ref, vb_ref,
                   oa_ref, ob_ref, m_sc, l_sc, acca_sc, s_sc, p_sc, a_sc, accb_sc, carry_sc, z_sc, w_sc, piece_sc,
                   *, blk, lambda_init):
    i = pl.program_id(2)
    a_start, a_trip, a_finish = _diff_parts(i, qa_ref, ka_ref, vta_ref, m_sc, l_sc, acca_sc, s_sc, p_sc, a_sc,
                                            blk=blk)
    b_start, b_trip, b_finish = _sb_parts(i, qb_ref, kb_ref, vb_ref, accb_sc, carry_sc, z_sc, w_sc, piece_sc,
                                          blk=blk)
    for a_stage, b_stage in zip(a_start, b_start):
        a_stage()
        b_stage()

    def trip(kb, cur):
        a_scores, a_softmax, a_acc = a_trip(kb, cur)
        b_scores, b_weights, b_acc = b_trip(kb, cur)
        for stage in (a_scores, b_weights, a_acc, b_scores, a_softmax, b_acc):
            stage()

    def loop_body(t, carry):
        for cur in range(2):
            pl.when(t % 2 == cur)(functools.partial(trip, i - 1 - t, cur))
        return carry

    lax.fori_loop(0, i, loop_body, 0)
    for a_stage, b_stage in zip(a_finish, b_finish):
        a_stage(lq1_ref, lk1_ref, lq2_ref, lk2_ref, g_ref, oa_ref, lambda_init)
        b_stage(ob_ref)


def _mixers(qa_t, ka, va_t, qb_t, kb_p, vb_p, lam_params, norm_g, *, blk, lambda_init):
    b, width, s = qa_t.shape
    hw = HEAD_COLS
    n_heads = width // hw
    nblk = s // blk
    vec_spec = pl.BlockSpec((1, DA_HEAD_DIM), lambda bi, h, i: (0, 0))
    q_spec = pl.BlockSpec((1, hw, blk), lambda bi, h, i: (bi, h, i))
    k_spec = pl.BlockSpec((1, s, hw), lambda bi, h, i: (bi, 0, h), pipeline_mode=pl.Buffered(1))
    v_spec = pl.BlockSpec((1, 1, nblk, hw, blk), lambda bi, h, i: (bi, h, 0, 0, 0), pipeline_mode=pl.Buffered(1))
    o_spec = pl.BlockSpec((1, blk, hw), lambda bi, h, i: (bi, i, h))
    kern = functools.partial(_mixers_kernel, blk=blk, lambda_init=lambda_init)
    out = jax.ShapeDtypeStruct((b, s, width), BF16)
    return pl.pallas_call(
        kern,
        out_shape=(out, out),
        grid=(b, n_heads, nblk),
        in_specs=[vec_spec, vec_spec, vec_spec, vec_spec,
                  pl.BlockSpec((1, hw), lambda bi, h, i: (0, 0)),
                  q_spec, k_spec, v_spec, q_spec, k_spec, k_spec],
        out_specs=(o_spec, o_spec),
        scratch_shapes=[pltpu.VMEM((2, SUBLANES, blk), F32),
                        pltpu.VMEM((2, SUBLANES, blk), F32),
                        pltpu.VMEM((2, hw, blk), F32),
                        pltpu.VMEM((2, 2, blk, blk), F32),
                        pltpu.VMEM((2, 2, blk, blk), BF16),
                        pltpu.VMEM((2, 2, SUBLANES, blk), F32),
                        pltpu.VMEM((blk, hw), F32),
                        pltpu.VMEM((SUBLANES, blk), F32),
                        pltpu.VMEM((2, blk, blk), F32),
                        pltpu.VMEM((2, blk, blk), BF16),
                        pltpu.VMEM((blk, blk), F32)],
        compiler_params=pltpu.CompilerParams(
            dimension_semantics=("parallel", "parallel", "arbitrary"),
            vmem_limit_bytes=_vmem_limit(58 * 1024 * 1024)),
        name="mixers",
    )(*lam_params, norm_g, qa_t, ka, va_t, qb_t, kb_p, vb_p)


def _merge_kernel(ya_ref, yb_ref, g_ref, x_ref, wa_ref, wb_ref, wo_ref, lg_ref, lb_ref, o_ref, *, alpha):
    d = x_ref.shape[1]
    pa = jnp.dot(ya_ref[...], wa_ref[...], preferred_element_type=F32)
    pb = jnp.dot(yb_ref[...], wb_ref[...], preferred_element_type=F32)
    mixed = g_ref[:, :d] * pa + g_ref[:, d:] * pb
    z = jnp.dot(mixed.astype(BF16), wo_ref[...], preferred_element_type=F32)
    o_ref[...] = _layer_norm(alpha * x_ref[...] + z, lg_ref[...], lb_ref[...])


def _merge(ya, yb, gates, x2d, wa, wb, wo, ln_g, ln_b, *, tm, alpha):
    t, d = x2d.shape
    row = lambda width: pl.BlockSpec((tm, width), lambda i: (i, 0))
    return pl.pallas_call(
        functools.partial(_merge_kernel, alpha=alpha),
        out_shape=jax.ShapeDtypeStruct((t, d), F32),
        grid=(t // tm,),
        in_specs=[row(ya.shape[1]), row(yb.shape[1]), row(gates.shape[1]), row(d),
                  _const_spec(wa.shape), _const_spec(wb.shape), _const_spec(wo.shape),
                  _const_spec(ln_g.shape), _const_spec(ln_b.shape)],
        out_specs=row(d),
        compiler_params=pltpu.CompilerParams(
            dimension_semantics=("parallel",),
            vmem_limit_bytes=_vmem_limit(58 * 1024 * 1024)),
        name="merge_ln1",
    )(ya, yb, gates, x2d, wa, wb, wo, ln_g, ln_b)


def _cross_kernel(x_ref, wq_ref, km_ref, vm_ref, wo_ref, lg_ref, lb_ref, o_ref, *, alpha, n_heads):
    x = x_ref[...]
    d = x.shape[1]
    dh = d // n_heads
    q = jnp.dot(x.astype(BF16), wq_ref[...], preferred_element_type=F32).astype(BF16)
    km = km_ref[0]
    vm = vm_ref[0]
    outs = []
    for h in range(n_heads):
        s = lax.dot_general(q[:, h * dh:(h + 1) * dh], km[:, h * dh:(h + 1) * dh],
                            (((1,), (1,)), ((), ())), preferred_element_type=F32) * (dh ** -0.5)
        s = s - jnp.max(s, axis=-1, keepdims=True)
        p = jnp.exp(s)
        p = p / jnp.sum(p, axis=-1, keepdims=True)
        outs.append(jnp.dot(p.astype(BF16), vm[:, h * dh:(h + 1) * dh], preferred_element_type=F32))
    o = jnp.concatenate(outs, axis=-1).astype(BF16)
    c = jnp.dot(o, wo_ref[...], preferred_element_type=F32)
    o_ref[...] = _layer_norm(alpha * x + c, lg_ref[...], lb_ref[...])


def _cross_attention(x2d, wq, kmem, vmem, wo, ln_g, ln_b, *, tm, seq, alpha):
    t, d = x2d.shape
    m = kmem.shape[1]
    per_seq = seq // tm
    row = pl.BlockSpec((tm, d), lambda i: (i, 0))
    mem_spec = pl.BlockSpec((1, m, d), lambda i: (i // per_seq, 0, 0))
    return pl.pallas_call(
        functools.partial(_cross_kernel, alpha=alpha, n_heads=MEM_HEADS),
        out_shape=jax.ShapeDtypeStruct((t, d), F32),
        grid=(t // tm,),
        in_specs=[row, _const_spec(wq.shape), mem_spec, mem_spec, _const_spec(wo.shape),
                  _const_spec(ln_g.shape), _const_spec(ln_b.shape)],
        out_specs=row,
        compiler_params=pltpu.CompilerParams(
            dimension_semantics=("parallel",),
            vmem_limit_bytes=_vmem_limit(58 * 1024 * 1024)),
        name="cross_ln2",
    )(x2d, wq, kmem, vmem, wo, ln_g, ln_b)


def _swiglu_kernel(x_ref, wg_ref, wu_ref, wd_ref, lg_ref, lb_ref, o_ref, xb_sc, acc_sc, *, alpha):
    j = pl.program_id(1)

    @pl.when(j == 0)
    def _():
        xb_sc[...] = x_ref[...].astype(BF16)
        acc_sc[...] = jnp.zeros(acc_sc.shape, F32)

    xb = xb_sc[...]
    g = jnp.dot(xb, wg_ref[...], preferred_element_type=F32)
    u = jnp.dot(xb, wu_ref[...], preferred_element_type=F32)
    hmid = (g * jax.nn.sigmoid(g) * u).astype(BF16)
    acc_sc[...] += jnp.dot(hmid, wd_ref[...], preferred_element_type=F32)

    @pl.when(j == pl.num_programs(1) - 1)
    def _():
        o_ref[...] = _layer_norm(alpha * x_ref[...] + acc_sc[...], lg_ref[...], lb_ref[...])


def _swiglu(x2d, wg, wu, wd, ln_g, ln_b, *, tm, tf, alpha):
    t, d = x2d.shape
    dff = wg.shape[1]
    row = pl.BlockSpec((tm, d), lambda i, j: (i, 0))
    vec = pl.BlockSpec((1, d), lambda i, j: (0, 0))
    return pl.pallas_call(
        functools.partial(_swiglu_kernel, alpha=alpha),
        out_shape=jax.ShapeDtypeStruct((t, d), F32),
        grid=(t // tm, dff // tf),
        in_specs=[row,
                  pl.BlockSpec((d, tf), lambda i, j: (0, j)),
                  pl.BlockSpec((d, tf), lambda i, j: (0, j)),
                  pl.BlockSpec((tf, d), lambda i, j: (j, 0)),
                  vec, vec],
        out_specs=row,
        scratch_shapes=[pltpu.VMEM((tm, d), BF16), pltpu.VMEM((tm, d), F32)],
        compiler_params=pltpu.CompilerParams(
            dimension_semantics=("parallel", "arbitrary"),
            vmem_limit_bytes=_vmem_limit(56 * 1024 * 1024)),
        name="swiglu_ln3",
    )(x2d, wg, wu, wd, ln_g, ln_b)


def _rope_tables(seq):
    inv = ROPE_THETA ** (-jnp.arange(0, DA_HEAD_DIM, 2, dtype=F32) / DA_HEAD_DIM)
    ang = jnp.arange(seq, dtype=F32)[:, None] * inv[None, :]
    cos, sin = jnp.cos(ang), jnp.sin(ang)
    return jnp.concatenate([cos, cos], axis=-1), jnp.concatenate([-sin, sin], axis=-1)


def kernel(x, mem, w_in, b_gate, lam_q1, lam_k1, lam_q2, lam_k2, da_norm_g, w_proj_a, w_proj_b, w_out,
           ln1_g, ln1_b, w_mq, w_mk, w_mv, w_mo, ln2_g, ln2_b, w_gate, w_up, w_down, ln3_g, ln3_b):
    bsz, seq, d = x.shape
    depth = w_in.shape[0]
    t = bsz * seq
    da_width = DA_HEADS * 2 * DA_HEAD_DIM
    sb_width = SB_HEADS * SB_HEAD_DIM
    alpha = (2.0 * depth) ** 0.25
    blk = min(ATTN_BLK, seq)
    rope = _rope_tables(seq)
    tm_proj = min(1024, seq)
    tm_tok = min(512, seq)
    tm_ff = min(512, seq)
    row_vec = lambda p: p.reshape(1, -1).astype(F32)

    x2d = x.reshape(t, d)
    for l in range(depth):
        lambda_init = 0.8 - 0.6 * math.exp(-0.3 * l)
        w_l = w_in[l].astype(BF16)
        proj_any = functools.partial(_projection, bsz=bsz, seq=seq, blk=blk, tm=tm_proj, tn=1024)
        bounds = [0, da_width, 2 * da_width, 3 * da_width, 3 * da_width + sb_width,
                  3 * da_width + 2 * sb_width, 3 * da_width + 3 * sb_width, w_l.shape[1]]
        w_qa, w_ka, w_va, w_qb, w_kb, w_vb, w_g = [w_l[:, lo:hi] for lo, hi in zip(bounds[:-1], bounds[1:])]
        qa_t, xb = proj_any(x2d, w_qa, name="proj_qa", rope=rope, scale=DA_HEAD_DIM ** -0.5 * LOG2E,
                            layout="cols", emit_x=True)
        proj = functools.partial(proj_any, xb)
        ka = proj(w_ka, name="proj_ka", rope=rope)
        va_t = proj(w_va, name="proj_va", layout="blocked")
        qb_t = proj(w_qb, name="proj_qb", scale=SB_HEAD_DIM ** -0.5 * LOG2E, layout="cols")
        kb_p = proj(w_kb, name="proj_kb", perm=True)
        vb_p = proj(w_vb, name="proj_vb", perm=True)
        gates = proj(w_g, name="proj_gate", bias=row_vec(b_gate[l]))

        ya, yb = _mixers(qa_t, ka.reshape(bsz, seq, da_width), va_t,
                         qb_t, kb_p.reshape(bsz, seq, sb_width), vb_p.reshape(bsz, seq, sb_width),
                         [row_vec(p[l]) for p in (lam_q1, lam_k1, lam_q2, lam_k2)],
                         row_vec(da_norm_g[l]), blk=blk, lambda_init=lambda_init)

        x2d = _merge(ya.reshape(t, da_width), yb.reshape(t, sb_width), gates, x2d,
                     w_proj_a[l].astype(BF16), w_proj_b[l].astype(BF16), w_out[l].astype(BF16),
                     row_vec(ln1_g[l]), row_vec(ln1_b[l]), tm=tm_tok, alpha=alpha)
        n_mem = mem.shape[1]
        mem2d = mem.reshape(bsz * n_mem, d).astype(BF16)
        w_kv = jnp.concatenate([w_mk[l], w_mv[l]], axis=1).astype(BF16)
        kv = _projection(mem2d, w_kv, name="proj_mem", bsz=bsz, seq=n_mem, blk=blk,
                         tm=min(256, n_mem), tn=1024)
        kmem = kv[:, :d].reshape(bsz, n_mem, d)
        vmem = kv[:, d:].reshape(bsz, n_mem, d)
        x2d = _cross_attention(x2d, w_mq[l].astype(BF16), kmem, vmem, w_mo[l].astype(BF16),
                               row_vec(ln2_g[l]), row_vec(ln2_b[l]), tm=tm_tok, seq=seq, alpha=alpha)
        x2d = _swiglu(x2d, w_gate[l].astype(BF16), w_up[l].astype(BF16), w_down[l].astype(BF16),
                      row_vec(ln3_g[l]), row_vec(ln3_b[l]), tm=tm_ff, tf=512, alpha=alpha)
    return x2d.reshape(bsz, seq, d)
```
